```python
import math
import jax, jax.numpy as jnp
from jax import lax
import numpy as np

D_MODEL = 1024
BATCH = 8
SEQ = 2048
DEPTH = 4
DEC_BATCH = 128
DEC_SEQ = 1
PAST_LEN = 16384
PAGE_SIZE = 128

N_EVEN = (DEPTH + 1) // 2
N_ODD = DEPTH // 2
HGRN_WIDTH = D_MODEL // 2
HGRN_EXPAND = 128
HGRN_HEADS = HGRN_WIDTH // HGRN_EXPAND
HGRN_DK = HGRN_EXPAND
HGRN_DV = HGRN_WIDTH // HGRN_HEADS
HGRN_CHUNK = 64
CONF_WIDTH = D_MODEL - HGRN_WIDTH
CONF_KERNEL = 31
SC_WIDTH = D_MODEL
SC_KERNEL = 3
D_FF = 256 * math.ceil(8 * D_MODEL / 3 / 256)
EVEN_IN = 4 * HGRN_WIDTH + 2 * CONF_WIDTH
ALPHA = (2 * DEPTH) ** 0.25
BETA = (8 * DEPTH) ** -0.25
LN_EPS = 1e-5
RMS_EPS = 1e-6

kernel_name = "hgrn2_conformer_shortconv_hybrid_step"


def _layernorm(x, w, b):
    xf = x.astype(jnp.float32)
    mu = jnp.mean(xf, axis=-1, keepdims=True)
    var = jnp.mean(jnp.square(xf - mu), axis=-1, keepdims=True)
    return ((xf - mu) * lax.rsqrt(var + LN_EPS) * w.astype(jnp.float32) + b.astype(jnp.float32)).astype(x.dtype)


def _causal_dwconv(u_ext, w):
    c = u_ext.shape[-1]
    return lax.conv_general_dilated(u_ext, w.astype(u_ext.dtype)[:, None, :], window_strides=(1,), padding='VALID',
                                    dimension_numbers=('NWC', 'WIO', 'NWC'), feature_group_count=c)


def _hgrn2_recurrence(q, k, v, g, s0):
    bsz, t_len, n_h = q.shape[:3]
    L = math.gcd(t_len, HGRN_CHUNK)
    n = t_len // L

    def to_chunks(a):
        return a.reshape(bsz, n, L, n_h, a.shape[-1]).transpose(1, 0, 3, 2, 4)

    tri = jnp.tril(jnp.ones((L, L), dtype=bool))

    def step(S, inp):
        qc, kc, vc, gc = inp
        b = jnp.cumsum(gc, axis=2)
        o = jnp.einsum('bhtk,bhkv->bhtv', qc * jnp.exp(b), S)
        diff = b[:, :, :, None, :] - b[:, :, None, :, :]
        decay = jnp.exp(jnp.where(tri[:, :, None], diff, -jnp.inf))
        scores = jnp.einsum('bhtk,bhsk,bhtsk->bhts', qc, kc, decay)
        o = o + jnp.einsum('bhts,bhsv->bhtv', scores, vc)
        b_last = b[:, :, -1:, :]
        S = jnp.exp(b_last[:, :, 0, :, None]) * S + jnp.einsum('bhsk,bhsv->bhkv', kc * jnp.exp(b_last - b), vc)
        return S, o

    s_final, o = lax.scan(step, s0.astype(jnp.float32), (to_chunks(q), to_chunks(k), to_chunks(v), to_chunks(g)))
    o = o.transpose(1, 0, 3, 2, 4).reshape(bsz, t_len, n_h, v.shape[-1])
    return o, s_final


def _even_mixer(x, s0, conf_buf, w_in, w_out, lb, gnorm_w, dw_w, dw_b, cln_w, cln_b):
    bsz, t_len, _ = x.shape
    f32 = jnp.float32
    p = x @ w_in
    hw = HGRN_WIDTH
    zq = p[..., 0:hw]
    zf = p[..., hw:2 * hw]
    vi = p[..., 2 * hw:3 * hw]
    zg = p[..., 3 * hw:4 * hw]
    ca = p[..., 4 * hw:4 * hw + CONF_WIDTH]
    cg = p[..., 4 * hw + CONF_WIDTH:]
    zf32 = zf.astype(f32)
    lb = lb.astype(f32)
    logf = jnp.logaddexp(jnp.log(lb), jnp.log1p(-lb) + jax.nn.log_sigmoid(zf32))
    kk = (1.0 - lb) * jax.nn.sigmoid(-zf32)
    qq = jax.nn.silu(zq.astype(f32)) * (HGRN_DK ** -0.5)
    shp_k = (bsz, t_len, HGRN_HEADS, HGRN_DK)
    o, s_new = _hgrn2_recurrence(qq.reshape(shp_k), kk.reshape(shp_k),
                                 vi.astype(f32).reshape(bsz, t_len, HGRN_HEADS, HGRN_DV),
                                 logf.reshape(shp_k), s0)
    o = o * lax.rsqrt(jnp.mean(jnp.square(o), axis=-1, keepdims=True) + RMS_EPS) * gnorm_w.astype(f32)
    o = o.reshape(bsz, t_len, HGRN_WIDTH) * jax.nn.silu(zg.astype(f32))
    u = ca * jax.nn.sigmoid(cg)
    u_ext = jnp.concatenate([conf_buf.astype(u.dtype), u], axis=1)
    c = _causal_dwconv(u_ext, dw_w) + dw_b
    c = jax.nn.silu(_layernorm(c, cln_w, cln_b))
    y = jnp.concatenate([o.astype(x.dtype), c.astype(x.dtype)], axis=-1) @ w_out
    return y, s_new.astype(s0.dtype), u_ext[:, -(CONF_KERNEL - 1):].astype(conf_buf.dtype)


def _odd_mixer(x, sc_buf, w_in, conv_w, w_out):
    p = x @ w_in
    bg = p[..., :SC_WIDTH]
    cg = p[..., SC_WIDTH:2 * SC_WIDTH]
    xv = p[..., 2 * SC_WIDTH:]
    z = cg * xv
    z_ext = jnp.concatenate([sc_buf.astype(z.dtype), z], axis=1)
    y = bg * _causal_dwconv(z_ext, conv_w)
    return y @ w_out, z_ext[:, -(SC_KERNEL - 1):].astype(sc_buf.dtype)


def _swiglu(x, w1, w3, w2):
    return (jax.nn.silu(x @ w1) * (x @ w3)) @ w2


def _trunk(x, s_hgrn, s_conf, s_sconv, lbs, w_in_even, w_out_even, hgrn_gnorm_w, conf_dw_w, conf_dw_b,
           conf_ln_w, conf_ln_b, sc_w_in, sc_conv_w, sc_w_out, ffn_w1, ffn_w3, ffn_w2,
           ln_mix_w, ln_mix_b, ln_ffn_w, ln_ffn_b):
    new_h, new_c, new_s = [], [], []
    for l in range(DEPTH):
        if l % 2 == 0:
            e = l // 2
            m, sh, cb = _even_mixer(x, s_hgrn[e], s_conf[e], w_in_even[e], w_out_even[e], lbs[e],
                                    hgrn_gnorm_w[e], conf_dw_w[e], conf_dw_b[e], conf_ln_w[e], conf_ln_b[e])
            new_h.append(sh)
            new_c.append(cb)
        else:
            o = l // 2
            m, sb = _odd_mixer(x, s_sconv[o], sc_w_in[o], sc_conv_w[o], sc_w_out[o])
            new_s.append(sb)
        x = _layernorm(ALPHA * x + m, ln_mix_w[l], ln_mix_b[l])
        x = _layernorm(ALPHA * x + _swiglu(x, ffn_w1[l], ffn_w3[l], ffn_w2[l]), ln_ffn_w[l], ln_ffn_b[l])
    return x, jnp.stack(new_h), jnp.stack(new_c), jnp.stack(new_s)


def setup_inputs(seed: int = 0) -> dict:
    key = jax.random.key(seed)
    ks = jax.random.split(key, 32)
    nrm = lambda k, shp, s: jax.random.normal(k, shp, jnp.float32) * s
    d = D_MODEL
    even_col_scale = jnp.concatenate([
        jnp.ones((2 * HGRN_WIDTH,)), jnp.full((HGRN_WIDTH,), BETA), jnp.ones((HGRN_WIDTH,)),
        jnp.full((CONF_WIDTH,), BETA), jnp.ones((CONF_WIDTH,))]).astype(jnp.float32)
    sc_col_scale = jnp.concatenate([jnp.ones((2 * SC_WIDTH,)), jnp.full((SC_WIDTH,), BETA)]).astype(jnp.float32)
    return {
        "x_prompt": nrm(ks[0], (BATCH, SEQ, d), 1.0),
        "x_sample": nrm(ks[1], (DEC_BATCH, DEC_SEQ, d), 1.0),
        "state_hgrn": nrm(ks[2], (N_EVEN, DEC_BATCH, HGRN_HEADS, HGRN_DK, HGRN_DV), 0.3),
        "state_conf": nrm(ks[3], (N_EVEN, DEC_BATCH, CONF_KERNEL - 1, CONF_WIDTH), 0.5),
        "state_sconv": nrm(ks[4], (N_ODD, DEC_BATCH, SC_KERNEL - 1, SC_WIDTH), 0.5),
        "w_in_even": nrm(ks[5], (N_EVEN, d, EVEN_IN), d ** -0.5) * even_col_scale,
        "w_out_even": nrm(ks[6], (N_EVEN, HGRN_WIDTH + CONF_WIDTH, d), (HGRN_WIDTH + CONF_WIDTH) ** -0.5 * BETA),
        "hgrn_lb_logits": nrm(ks[7], (N_EVEN, HGRN_WIDTH), 0.1),
        "hgrn_gnorm_w": 1.0 + nrm(ks[8], (N_EVEN, HGRN_DV), 0.05),
        "conf_dw_w": nrm(ks[9], (N_EVEN, CONF_KERNEL, CONF_WIDTH), CONF_KERNEL ** -0.5),
        "conf_dw_b": nrm(ks[10], (N_EVEN, CONF_WIDTH), 0.02),
        "conf_ln_w": 1.0 + nrm(ks[11], (N_EVEN, CONF_WIDTH), 0.05),
        "conf_ln_b": nrm(ks[12], (N_EVEN, CONF_WIDTH), 0.02),
        "sc_w_in": nrm(ks[13], (N_ODD, d, 3 * SC_WIDTH), d ** -0.5) * sc_col_scale,
        "sc_conv_w": nrm(ks[14], (N_ODD, SC_KERNEL, SC_WIDTH), SC_KERNEL ** -0.5),
        "sc_w_out": nrm(ks[15], (N_ODD, SC_WIDTH, d), SC_WIDTH ** -0.5 * BETA),
        "ffn_w1": nrm(ks[16], (DEPTH, d, D_FF), d ** -0.5),
        "ffn_w3": nrm(ks[17], (DEPTH, d, D_FF), d ** -0.5 * BETA),
        "ffn_w2": nrm(ks[18], (DEPTH, D_FF, d), D_FF ** -0.5 * BETA),
        "ln_mix_w": 1.0 + nrm(ks[19], (DEPTH, d), 0.05),
        "ln_mix_b": nrm(ks[20], (DEPTH, d), 0.02),
        "ln_ffn_w": 1.0 + nrm(ks[21], (DEPTH, d), 0.05),
        "ln_ffn_b": nrm(ks[22], (DEPTH, d), 0.02),
    }


def reference(x_prompt, x_sample, state_hgrn, state_conf, state_sconv, w_in_even, w_out_even, hgrn_lb_logits,
              hgrn_gnorm_w, conf_dw_w, conf_dw_b, conf_ln_w, conf_ln_b, sc_w_in, sc_conv_w, sc_w_out,
              ffn_w1, ffn_w3, ffn_w2, ln_mix_w, ln_mix_b, ln_ffn_w, ln_ffn_b):
    lbs = jnp.cumsum(jax.nn.softmax(hgrn_lb_logits.astype(jnp.float32), axis=0), axis=0)
    lbs = lbs - lbs[0:1]
    weights = (w_in_even, w_out_even, hgrn_gnorm_w, conf_dw_w, conf_dw_b, conf_ln_w, conf_ln_b,
               sc_w_in, sc_conv_w, sc_w_out, ffn_w1, ffn_w3, ffn_w2, ln_mix_w, ln_mix_b, ln_ffn_w, ln_ffn_b)
    zh = jnp.zeros((N_EVEN, BATCH, HGRN_HEADS, HGRN_DK, HGRN_DV), state_hgrn.dtype)
    zc = jnp.zeros((N_EVEN, BATCH, CONF_KERNEL - 1, CONF_WIDTH), state_conf.dtype)
    zs = jnp.zeros((N_ODD, BATCH, SC_KERNEL - 1, SC_WIDTH), state_sconv.dtype)
    y_prompt, h_p, c_p, s_p = _trunk(x_prompt, zh, zc, zs, lbs, *weights)
    y_sample, h_s, c_s, s_s = _trunk(x_sample, state_hgrn, state_conf, state_sconv, lbs, *weights)
    return (y_prompt, y_sample, h_p, c_p, s_p, h_s, c_s, s_s)
```

```python
import functools
import math

import numpy as np
import jax
import jax.numpy as jnp
from jax import lax
from jax.experimental import pallas as pl
from jax.experimental.pallas import tpu as pltpu

F32 = jnp.float32
BF16 = jnp.bfloat16

D_MODEL = 1024
DEPTH = 4
HGRN_WIDTH = D_MODEL // 2
HGRN_DK = 128
HGRN_DV = 128
HGRN_HEADS = HGRN_WIDTH // HGRN_DK
CONF_WIDTH = D_MODEL - HGRN_WIDTH
CONF_KERNEL = 31
SC_WIDTH = D_MODEL
SC_KERNEL = 3
D_FF = 256 * math.ceil(8 * D_MODEL / 3 / 256)
EVEN_IN = 4 * HGRN_WIDTH + 2 * CONF_WIDTH
ALPHA = (2 * DEPTH) ** 0.25
LN_EPS = 1e-5
RMS_EPS = 1e-6

SUBLANES = 8
CHUNK = 128
N_LEVELS = CHUNK.bit_length() - 1
TB_EVEN = 256
TB_ODD = 512
TM_FFN = 512
CONF_HIST = 32
SC_HIST = SUBLANES
SAMPLE_GROUP = 8
VMEM_LIMIT = 56 * 1024 * 1024


def _sigmoid(x):
    return 1.0 / (1.0 + jnp.exp(-x))


def _silu(x):
    return x * _sigmoid(x)


def _layernorm(x, w, b):
    mu = jnp.mean(x, axis=-1, keepdims=True)
    xc = x - mu
    var = jnp.mean(xc * xc, axis=-1, keepdims=True)
    return xc * lax.rsqrt(var + LN_EPS) * w + b


def _dot(a, b):
    return jnp.dot(a, b, preferred_element_type=F32)


def _dot_nt(a, b):
    return lax.dot_general(a, b, (((1,), (1,)), ((), ())), preferred_element_type=F32)


def _dot_tn(a, b):
    return lax.dot_general(a, b, (((0,), (0,)), ((), ())), preferred_element_type=F32)


def _split3(x):
    hi = x.astype(BF16)
    r1 = x - hi.astype(F32)
    mid = r1.astype(BF16)
    lo = (r1 - mid.astype(F32)).astype(BF16)
    return hi, mid, lo


def _lower_bound(logits, e):
    m = jnp.max(logits, axis=0, keepdims=True)
    ex = jnp.exp(logits - m)
    sm = ex / jnp.sum(ex, axis=0, keepdims=True)
    lb = jnp.zeros_like(m)
    for i in range(1, e + 1):
        lb = lb + sm[i:i + 1]
    return lb


def _hgrn_gates(zq, zf, lb):
    log_sig = jnp.minimum(zf, 0.0) - jnp.log1p(jnp.exp(-jnp.abs(zf)))
    a = jnp.log(lb)
    c = jnp.log1p(-lb) + log_sig
    logf = jnp.maximum(a, c) + jnp.log1p(jnp.exp(-jnp.abs(a - c)))
    kk = (1.0 - lb) * _sigmoid(-zf)
    qq = _silu(zq) * (HGRN_DK ** -0.5)
    return qq, kk, logf


def _chunk_constants():
    L = CHUNK
    t = np.arange(L)[:, None]
    r = np.arange(L)[None, :]
    mats = []
    for j in range(1, N_LEVELS + 1):
        half = 1 << (j - 1)
        start = (t >> j) << j
        mid = start + half - 1
        upper = t >= start + half
        d = np.where(upper, (r > mid) & (r <= t), (r > t) & (r <= mid))
        mats.append(d)
    mats.append(r <= t)
    mats.append(r > t)
    d_all = np.concatenate(mats, axis=0).astype(np.float32)
    x = np.bitwise_xor(t, r)
    lvl = np.zeros((L, L), np.int32)
    nz = x > 0
    lvl[nz] = np.floor(np.log2(x[nz])).astype(np.int32) + 1
    lvl = np.where(r > t, -1, lvl).astype(np.int32)
    return jnp.asarray(d_all, dtype=BF16), jnp.asarray(lvl)


def _even_prompt_kernel(e, n_t, x_ref, w_in_ref, w_out_ref, logit_ref, gn_ref, dww_ref, dwb_ref, clw_ref, clb_ref,
                        lnw_ref, lnb_ref, dall_ref, lvl_ref, y_ref, s_out_ref, conf_out_ref, st_ref, ubuf_ref):
    t_idx = pl.program_id(1)
    tb = x_ref.shape[0]
    hw = HGRN_WIDTH

    @pl.when(t_idx == 0)
    def _():
        st_ref[...] = jnp.zeros_like(st_ref)
        ubuf_ref[0:CONF_HIST, :] = jnp.zeros((CONF_HIST, CONF_WIDTH), F32)

    x = x_ref[...]
    p = _dot(x.astype(BF16), w_in_ref[...])
    zq = p[:, 0:hw]
    zf = p[:, hw:2 * hw]
    vi = p[:, 2 * hw:3 * hw]
    zg = p[:, 3 * hw:4 * hw]
    ca = p[:, 4 * hw:4 * hw + CONF_WIDTH]
    cg = p[:, 4 * hw + CONF_WIDTH:]

    lb = _lower_bound(logit_ref[...], e)
    qq, kk, logf = _hgrn_gates(zq, zf, lb)
    lvl = lvl_ref[...]
    dall = dall_ref[...]
    gn = gn_ref[...]
    L = CHUNK
    o_chunks = []
    for c in range(tb // L):
        rows = slice(c * L, (c + 1) * L)
        ghi, gmid, glo = _split3(logf[rows])
        ee = jnp.exp(_dot(dall, ghi) + _dot(dall, gmid) + _dot(dall, glo))
        o_heads = []
        for h in range(HGRN_HEADS):
            cols = slice(h * HGRN_DK, (h + 1) * HGRN_DK)
            q_h = qq[rows, cols]
            k_h = kk[rows, cols]
            v_h = vi[rows, cols].astype(BF16)
            scores = jnp.where(lvl == 0, _dot_nt(q_h.astype(BF16), k_h.astype(BF16)), 0.0)
            for j in range(1, N_LEVELS + 1):
                e_j = ee[(j - 1) * L:j * L, cols]
                pj = _dot_nt((q_h * e_j).astype(BF16), (k_h * e_j).astype(BF16))
                scores = jnp.where(lvl == j, pj, scores)
            e_q = ee[N_LEVELS * L:(N_LEVELS + 1) * L, cols]
            e_k = ee[(N_LEVELS + 1) * L:(N_LEVELS + 2) * L, cols]
            st = st_ref[h]
            o_h = _dot(scores.astype(BF16), v_h) + _dot_nt((q_h * e_q).astype(BF16), st.astype(BF16))
            st_ref[h] = e_q[L - 1:L, :] * st + _dot_tn(v_h, (k_h * e_k).astype(BF16))
            o_h = o_h * lax.rsqrt(jnp.mean(o_h * o_h, axis=-1, keepdims=True) + RMS_EPS) * gn
            o_heads.append(o_h)
        o_chunks.append(jnp.concatenate(o_heads, axis=-1))
    o = jnp.concatenate(o_chunks, axis=0) * _silu(zg)

    u = ca * _sigmoid(cg)
    ubuf_ref[CONF_HIST:CONF_HIST + tb, :] = u
    off = CONF_HIST - (CONF_KERNEL - 1)
    acc = jnp.broadcast_to(dwb_ref[...], (tb, CONF_WIDTH))
    for j in range(CONF_KERNEL):
        acc = acc + dww_ref[j:j + 1, :] * ubuf_ref[off + j:off + j + tb, :]
    cc = _silu(_layernorm(acc, clw_ref[...], clb_ref[...]))

    @pl.when(t_idx == n_t - 1)
    def _():
        conf_out_ref[...] = ubuf_ref[tb + off:tb + CONF_HIST, :]
        for h in range(HGRN_HEADS):
            s_out_ref[h] = st_ref[h].T

    ubuf_ref[0:CONF_HIST, :] = ubuf_ref[tb:tb + CONF_HIST, :]

    y = _dot(jnp.concatenate([o, cc], axis=-1).astype(BF16), w_out_ref[...])
    y_ref[...] = _layernorm(ALPHA * x + y, lnw_ref[...], lnb_ref[...])


def _even_prompt(x, e, layer, w_in, w_out, logits, gn, dww, dwb, clw, clb, lnw, lnb, dall, lvl):
    bsz, t_len, d = x.shape
    tb = min(TB_EVEN, t_len)
    assert t_len % tb == 0 and tb % CHUNK == 0 and t_len >= CONF_KERNEL - 1
    n_t = t_len // tb
    const2 = lambda b, t: (0, 0)
    return pl.pallas_call(
        functools.partial(_even_prompt_kernel, e, n_t),
        grid=(bsz, n_t),
        in_specs=[
            pl.BlockSpec((None, tb, d), lambda b, t: (b, t, 0)),
            pl.BlockSpec((None, d, EVEN_IN), lambda b, t: (e, 0, 0)),
            pl.BlockSpec((None, d, d), lambda b, t: (e, 0, 0)),
            pl.BlockSpec(logits.shape, const2),
            pl.BlockSpec((None, 1, HGRN_DV), lambda b, t: (e, 0, 0)),
            pl.BlockSpec((None, CONF_KERNEL, CONF_WIDTH), lambda b, t: (e, 0, 0)),
            pl.BlockSpec((None, 1, CONF_WIDTH), lambda b, t: (e, 0, 0)),
            pl.BlockSpec((None, 1, CONF_WIDTH), lambda b, t: (e, 0, 0)),
            pl.BlockSpec((None, 1, CONF_WIDTH), lambda b, t: (e, 0, 0)),
            pl.BlockSpec((None, 1, d), lambda b, t: (layer, 0, 0)),
            pl.BlockSpec((None, 1, d), lambda b, t: (layer, 0, 0)),
            pl.BlockSpec(dall.shape, const2),
            pl.BlockSpec(lvl.shape, const2),
        ],
        out_specs=[
            pl.BlockSpec((None, tb, d), lambda b, t: (b, t, 0)),
            pl.BlockSpec((None, HGRN_HEADS, HGRN_DK, HGRN_DV), lambda b, t: (b, 0, 0, 0)),
            pl.BlockSpec((None, CONF_KERNEL - 1, CONF_WIDTH), lambda b, t: (b, 0, 0)),
        ],
        out_shape=[
            jax.ShapeDtypeStruct((bsz, t_len, d), F32),
            jax.ShapeDtypeStruct((bsz, HGRN_HEADS, HGRN_DK, HGRN_DV), F32),
            jax.ShapeDtypeStruct((bsz, CONF_KERNEL - 1, CONF_WIDTH), F32),
        ],
        scratch_shapes=[
            pltpu.VMEM((HGRN_HEADS, HGRN_DV, HGRN_DK), F32),
            pltpu.VMEM((CONF_HIST + tb, CONF_WIDTH), F32),
        ],
        compiler_params=pltpu.CompilerParams(dimension_semantics=("arbitrary", "arbitrary"),
                                             vmem_limit_bytes=VMEM_LIMIT),
        name=f"even_prompt_{e}",
    )(x, w_in, w_out, logits, gn, dww, dwb, clw, clb, lnw, lnb, dall, lvl)


def _even_sample_kernel(e, n_g, aliased, x_ref, w_in_ref, w_out_ref, logit_ref, gn_ref, dww_ref, dwb_ref, clw_ref,
                        clb_ref, lnw_ref, lnb_ref, s_ref, c_ref, *rest):
    if aliased:
        rest = rest[2:]
    y_ref, s_out_ref, c_out_ref, qt_ref, at_ref, kt_ref, v_ref, gate_ref, u_ref, o_ref, cacc_ref = rest
    i = pl.program_id(0)
    hw = HGRN_WIDTH
    n_seq = x_ref.shape[0]

    @pl.when(i == 0)
    def _():
        p = _dot(x_ref[...].astype(BF16), w_in_ref[...])
        lb = _lower_bound(logit_ref[...], e)
        qq, kk, logf = _hgrn_gates(p[:, 0:hw], p[:, hw:2 * hw], lb)
        qt_ref[...] = qq.T
        kt_ref[...] = kk.T
        at_ref[...] = jnp.exp(logf).T
        v_ref[...] = p[:, 2 * hw:3 * hw]
        gate_ref[...] = _silu(p[:, 3 * hw:4 * hw])
        u_ref[...] = p[:, 4 * hw:4 * hw + CONF_WIDTH] * _sigmoid(p[:, 4 * hw + CONF_WIDTH:])

    lane = lax.broadcasted_iota(jnp.int32, (HGRN_DK, n_seq), 1)
    n_hist = CONF_KERNEL - 1
    w_hist = dww_ref[0:n_hist, :]
    w_last = dww_ref[n_hist:n_hist + 1, :]
    grp = pl.ds(pl.multiple_of(i * SAMPLE_GROUP, SAMPLE_GROUP), SAMPLE_GROUP)
    v_grp = v_ref[grp, :]
    u_grp = u_ref[grp, :]
    o_rows, c_rows = [], []
    for j in range(SAMPLE_GROUP):
        pick = lane == i * SAMPLE_GROUP + j
        o_heads = []
        for h in range(HGRN_HEADS):
            cols = slice(h * HGRN_DK, (h + 1) * HGRN_DK)
            a_col = jnp.sum(jnp.where(pick, at_ref[cols, :], 0.0), axis=1, keepdims=True)
            k_col = jnp.sum(jnp.where(pick, kt_ref[cols, :], 0.0), axis=1, keepdims=True)
            q_col = jnp.sum(jnp.where(pick, qt_ref[cols, :], 0.0), axis=1, keepdims=True)
            s_new = a_col * s_ref[j, h] + k_col * v_grp[j:j + 1, cols]
            s_out_ref[j, h] = s_new
            o_heads.append(jnp.sum(q_col * s_new, axis=0, keepdims=True))
        o_rows.append(jnp.concatenate(o_heads, axis=-1))
        buf = c_ref[j]
        u_row = u_grp[j:j + 1, :]
        c_rows.append(jnp.sum(buf * w_hist, axis=0, keepdims=True) + w_last * u_row + dwb_ref[...])
        c_out_ref[j, 0:n_hist - 1, :] = buf[1:n_hist, :]
        c_out_ref[j, n_hist - 1:n_hist, :] = u_row
    o_ref[grp, :] = jnp.concatenate(o_rows, axis=0)
    cacc_ref[grp, :] = jnp.concatenate(c_rows, axis=0)

    @pl.when(i == n_g - 1)
    def _():
        gn = gn_ref[...]
        o_heads = []
        for h in range(HGRN_HEADS):
            o_h = o_ref[:, h * HGRN_DK:(h + 1) * HGRN_DK]
            o_heads.append(o_h * lax.rsqrt(jnp.mean(o_h * o_h, axis=-1, keepdims=True) + RMS_EPS) * gn)
        o = jnp.concatenate(o_heads, axis=-1) * gate_ref[...]
        cc = _silu(_layernorm(cacc_ref[...], clw_ref[...], clb_ref[...]))
        y = _dot(jnp.concatenate([o, cc], axis=-1).astype(BF16), w_out_ref[...])
        y_ref[...] = _layernorm(ALPHA * x_ref[...] + y, lnw_ref[...], lnb_ref[...])


def _even_sample(x, e, layer, w_in, w_out, logits, gn, dww, dwb, clw, clb, lnw, lnb, state_hgrn, state_conf,
                 prev_h=None, prev_c=None):
    n_seq, d = x.shape
    g = SAMPLE_GROUP
    assert n_seq % g == 0 and n_seq % 128 == 0
    n_g = n_seq // g
    aliased = prev_h is not None
    const2 = lambda i: (0, 0)
    in_specs = [
        pl.BlockSpec((n_seq, d), const2),
        pl.BlockSpec((None, d, EVEN_IN), lambda i: (e, 0, 0)),
        pl.BlockSpec((None, d, d), lambda i: (e, 0, 0)),
        pl.BlockSpec(logits.shape, const2),
        pl.BlockSpec((None, 1, HGRN_DV), lambda i: (e, 0, 0)),
        pl.BlockSpec((None, CONF_KERNEL, CONF_WIDTH), lambda i: (e, 0, 0)),
        pl.BlockSpec((None, 1, CONF_WIDTH), lambda i: (e, 0, 0)),
        pl.BlockSpec((None, 1, CONF_WIDTH), lambda i: (e, 0, 0)),
        pl.BlockSpec((None, 1, CONF_WIDTH), lambda i: (e, 0, 0)),
        pl.BlockSpec((None, 1, d), lambda i: (layer, 0, 0)),
        pl.BlockSpec((None, 1, d), lambda i: (layer, 0, 0)),
        pl.BlockSpec((None, g, HGRN_HEADS, HGRN_DK, HGRN_DV), lambda i: (e, i, 0, 0, 0)),
        pl.BlockSpec((None, g, CONF_KERNEL - 1, CONF_WIDTH), lambda i: (e, i, 0, 0)),
    ]
    args = [x, w_in, w_out, logits, gn, dww, dwb, clw, clb, lnw, lnb, state_hgrn, state_conf]
    aliases = {}
    if aliased:
        in_specs += [pl.BlockSpec(memory_space=pl.ANY), pl.BlockSpec(memory_space=pl.ANY)]
        aliases = {len(args): 1, len(args) + 1: 2}
        args += [prev_h, prev_c]
    return pl.pallas_call(
        functools.partial(_even_sample_kernel, e, n_g, aliased),
        grid=(n_g,),
        in_specs=in_specs,
        out_specs=[
            pl.BlockSpec((n_seq, d), const2),
            pl.BlockSpec((None, g, HGRN_HEADS, HGRN_DK, HGRN_DV), lambda i: (e, i, 0, 0, 0)),
            pl.BlockSpec((None, g, CONF_KERNEL - 1, CONF_WIDTH), lambda i: (e, i, 0, 0)),
        ],
        out_shape=[
            jax.ShapeDtypeStruct((n_seq, d), F32),
            jax.ShapeDtypeStruct(state_hgrn.shape, F32),
            jax.ShapeDtypeStruct(state_conf.shape, F32),
        ],
        scratch_shapes=[
            pltpu.VMEM((HGRN_WIDTH, n_seq), F32),
            pltpu.VMEM((HGRN_WIDTH, n_seq), F32),
            pltpu.VMEM((HGRN_WIDTH, n_seq), F32),
            pltpu.VMEM((n_seq, HGRN_WIDTH), F32),
            pltpu.VMEM((n_seq, HGRN_WIDTH), F32),
            pltpu.VMEM((n_seq, CONF_WIDTH), F32),
            pltpu.VMEM((n_seq, HGRN_WIDTH), F32),
            pltpu.VMEM((n_seq, CONF_WIDTH), F32),
        ],
        input_output_aliases=aliases,
        compiler_params=pltpu.CompilerParams(dimension_semantics=("arbitrary",), vmem_limit_bytes=VMEM_LIMIT),
        name=f"even_sample_{e}",
    )(*args)


def _odd_prompt_kernel(n_t, x_ref, w_in_ref, w_out_ref, cw_ref, lnw_ref, lnb_ref, y_ref, z_out_ref, zbuf_ref):
    t_idx = pl.program_id(1)
    tb = x_ref.shape[0]
    w = SC_WIDTH

    @pl.when(t_idx == 0)
    def _():
        zbuf_ref[0:SC_HIST, :] = jnp.zeros((SC_HIST, w), F32)

    x = x_ref[...]
    p = _dot(x.astype(BF16), w_in_ref[...])
    bg = p[:, 0:w]
    z = p[:, w:2 * w] * p[:, 2 * w:3 * w]
    zbuf_ref[SC_HIST:SC_HIST + tb, :] = z
    off = SC_HIST - (SC_KERNEL - 1)
    conv = cw_ref[SC_KERNEL - 1:SC_KERNEL, :] * z
    for j in range(SC_KERNEL - 1):
        conv = conv + cw_ref[j:j + 1, :] * zbuf_ref[off + j:off + j + tb, :]

    @pl.when(t_idx == n_t - 1)
    def _():
        z_out_ref[...] = zbuf_ref[tb + off:tb + SC_HIST, :]

    zbuf_ref[0:SC_HIST, :] = zbuf_ref[tb:tb + SC_HIST, :]
    y = _dot((bg * conv).astype(BF16), w_out_ref[...])
    y_ref[...] = _layernorm(ALPHA * x + y, lnw_ref[...], lnb_ref[...])


def _odd_prompt(x, o, layer, w_in, w_out, cw, lnw, lnb):
    bsz, t_len, d = x.shape
    tb = min(TB_ODD, t_len)
    assert t_len % tb == 0 and t_len >= SC_KERNEL - 1
    n_t = t_len // tb
    return pl.pallas_call(
        functools.partial(_odd_prompt_kernel, n_t),
        grid=(bsz, n_t),
        in_specs=[
            pl.BlockSpec((None, tb, d), lambda b, t: (b, t, 0)),
            pl.BlockSpec((None, d, 3 * SC_WIDTH), lambda b, t: (o, 0, 0)),
            pl.BlockSpec((None, SC_WIDTH, d), lambda b, t: (o, 0, 0)),
            pl.BlockSpec((None, SC_KERNEL, SC_WIDTH), lambda b, t: (o, 0, 0)),
            pl.BlockSpec((None, 1, d), lambda b, t: (layer, 0, 0)),
            pl.BlockSpec((None, 1, d), lambda b, t: (layer, 0, 0)),
        ],
        out_specs=[
            pl.BlockSpec((None, tb, d), lambda b, t: (b, t, 0)),
            pl.BlockSpec((None, SC_KERNEL - 1, SC_WIDTH), lambda b, t: (b, 0, 0)),
        ],
        out_shape=[
            jax.ShapeDtypeStruct((bsz, t_len, d), F32),
            jax.ShapeDtypeStruct((bsz, SC_KERNEL - 1, SC_WIDTH), F32),
        ],
        scratch_shapes=[pltpu.VMEM((SC_HIST + tb, SC_WIDTH), F32)],
        compiler_params=pltpu.CompilerParams(dimension_semantics=("arbitrary", "arbitrary"),
                                             vmem_limit_bytes=VMEM_LIMIT),
        name=f"odd_prompt_{o}",
    )(x, w_in, w_out, cw, lnw, lnb)


def _odd_sample_kernel(x_ref, w_in_ref, w_out_ref, cw_ref, lnw_ref, lnb_ref, s_ref, y_ref, s_out_ref):
    w = SC_WIDTH
    x = x_ref[...]
    p = _dot(x.astype(BF16), w_in_ref[...])
    z = p[:, w:2 * w] * p[:, 2 * w:3 * w]
    s0 = s_ref[:, 0:w]
    s1 = s_ref[:, w:2 * w]
    conv = cw_ref[0:1, :] * s0 + cw_ref[1:2, :] * s1 + cw_ref[2:3, :] * z
    s_out_ref[:, 0:w] = s1
    s_out_ref[:, w:2 * w] = z
    y = _dot((p[:, 0:w] * conv).astype(BF16), w_out_ref[...])
    y_ref[...] = _layernorm(ALPHA * x + y, lnw_ref[...], lnb_ref[...])


def _odd_sample(x, o, layer, w_in, w_out, cw, lnw, lnb, state):
    n_seq, d = x.shape
    assert SC_KERNEL == 3
    hist = (SC_KERNEL - 1) * SC_WIDTH
    return pl.pallas_call(
        _odd_sample_kernel,
        grid=(1,),
        in_specs=[
            pl.BlockSpec((n_seq, d), lambda i: (0, 0)),
            pl.BlockSpec((None, d, 3 * SC_WIDTH), lambda i: (o, 0, 0)),
            pl.BlockSpec((None, SC_WIDTH, d), lambda i: (o, 0, 0)),
            pl.BlockSpec((None, SC_KERNEL, SC_WIDTH), lambda i: (o, 0, 0)),
            pl.BlockSpec((None, 1, d), lambda i: (layer, 0, 0)),
            pl.BlockSpec((None, 1, d), lambda i: (layer, 0, 0)),
            pl.BlockSpec((None, n_seq, hist), lambda i: (o, 0, 0)),
        ],
        out_specs=[
            pl.BlockSpec((n_seq, d), lambda i: (0, 0)),
            pl.BlockSpec((n_seq, hist), lambda i: (0, 0)),
        ],
        out_shape=[
            jax.ShapeDtypeStruct((n_seq, d), F32),
            jax.ShapeDtypeStruct((n_seq, hist), F32),
        ],
        compiler_params=pltpu.CompilerParams(dimension_semantics=("arbitrary",), vmem_limit_bytes=VMEM_LIMIT),
        name=f"odd_sample_{o}",
    )(x, w_in, w_out, cw, lnw, lnb, state)


def _ffn_kernel(x_ref, w1_ref, w3_ref, w2_ref, lnw_ref, lnb_ref, y_ref):
    x = x_ref[...]
    xb = x.astype(BF16)
    h1 = _dot(xb, w1_ref[...])
    h3 = _dot(xb, w3_ref[...])
    y = _dot((_silu(h1) * h3).astype(BF16), w2_ref[...])
    y_ref[...] = _layernorm(ALPHA * x + y, lnw_ref[...], lnb_ref[...])


def _ffn(x, layer, w1, w3, w2, lnw, lnb):
    m, d = x.shape
    tm = min(TM_FFN, m)
    assert m % tm == 0
    return pl.pallas_call(
        _ffn_kernel,
        grid=(m // tm,),
        in_specs=[
            pl.BlockSpec((tm, d), lambda i: (i, 0)),
            pl.BlockSpec((None, d, D_FF), lambda i: (layer, 0, 0)),
            pl.BlockSpec((None, d, D_FF), lambda i: (layer, 0, 0)),
            pl.BlockSpec((None, D_FF, d), lambda i: (layer, 0, 0)),
            pl.BlockSpec((None, 1, d), lambda i: (layer, 0, 0)),
            pl.BlockSpec((None, 1, d), lambda i: (layer, 0, 0)),
        ],
        out_specs=pl.BlockSpec((tm, d), lambda i: (i, 0)),
        out_shape=jax.ShapeDtypeStruct((m, d), F32),
        compiler_params=pltpu.CompilerParams(dimension_semantics=("arbitrary",), vmem_limit_bytes=VMEM_LIMIT),
        name=f"ffn_{layer}",
    )(x, w1, w3, w2, lnw, lnb)


def kernel(x_prompt, x_sample, state_hgrn, state_conf, state_sconv, w_in_even, w_out_even, hgrn_lb_logits,
           hgrn_gnorm_w, conf_dw_w, conf_dw_b, conf_ln_w, conf_ln_b, sc_w_in, sc_conv_w, sc_w_out,
           ffn_w1, ffn_w3, ffn_w2, ln_mix_w, ln_mix_b, ln_ffn_w, ln_ffn_b):
    bsz, t_len, d = x_prompt.shape
    n_seq = x_sample.shape[0]
    n_even, n_odd = state_hgrn.shape[0], state_sconv.shape[0]
    assert d == D_MODEL and x_sample.shape[1] == 1

    w_in_even = w_in_even.astype(BF16)
    w_out_even = w_out_even.astype(BF16)
    sc_w_in = sc_w_in.astype(BF16)
    sc_w_out = sc_w_out.astype(BF16)
    ffn_w1 = ffn_w1.astype(BF16)
    ffn_w3 = ffn_w3.astype(BF16)
    ffn_w2 = ffn_w2.astype(BF16)
    row = lambda a: a.reshape(a.shape[0], 1, a.shape[1])
    gn, dwb, clw, clb = row(hgrn_gnorm_w), row(conf_dw_b), row(conf_ln_w), row(conf_ln_b)
    lmw, lmb, lfw, lfb = row(ln_mix_w), row(ln_mix_b), row(ln_ffn_w), row(ln_ffn_b)
    sconv_flat = state_sconv.reshape(n_odd, n_seq, (SC_KERNEL - 1) * SC_WIDTH)
    dall, lvl = _chunk_constants()

    xp = x_prompt
    xs = x_sample.reshape(n_seq, d)
    h_p, c_p, s_p, s_s = [], [], [], []
    h_s = c_s = None
    for layer in range(DEPTH):
        if layer % 2 == 0:
            e = layer // 2
            xp, sh, cb = _even_prompt(xp, e, layer, w_in_even, w_out_even, hgrn_lb_logits, gn, conf_dw_w, dwb,
                                      clw, clb, lmw, lmb, dall, lvl)
            h_p.append(sh)
            c_p.append(cb)
            xs, h_s, c_s = _even_sample(xs, e, layer, w_in_even, w_out_even, hgrn_lb_logits, gn, conf_dw_w, dwb,
                                        clw, clb, lmw, lmb, state_hgrn, state_conf, h_s, c_s)
        else:
            o = layer // 2
            xp, sb = _odd_prompt(xp, o, layer, sc_w_in, sc_w_out, sc_conv_w, lmw, lmb)
            s_p.append(sb)
            xs, sb = _odd_sample(xs, o, layer, sc_w_in, sc_w_out, sc_conv_w, lmw, lmb, sconv_flat)
            s_s.append(sb.reshape(n_seq, SC_KERNEL - 1, SC_WIDTH))
        xp = _ffn(xp.reshape(bsz * t_len, d), layer, ffn_w1, ffn_w3, ffn_w2, lfw, lfb).reshape(bsz, t_len, d)
        xs = _ffn(xs, layer, ffn_w1, ffn_w3, ffn_w2, lfw, lfb)
    return (xp, xs.reshape(n_seq, 1, d), jnp.stack(h_p), jnp.stack(c_p), jnp.stack(s_p), h_s, c_s,
            jnp.stack(s_s))
```

```python
import functools
import math

import numpy as np
import jax
import jax.numpy as jnp
from jax import lax
from jax.experimental import pallas as pl
from jax.experimental.pallas import tpu as pltpu

F32 = jnp.float32
BF16 = jnp.bfloat16

D_MODEL = 1024
DEPTH = 4
HGRN_WIDTH = D_MODEL // 2
HGRN_DK = 128
HGRN_DV = 128
HGRN_HEADS = HGRN_WIDTH // HGRN_DK
CONF_WIDTH = D_MODEL - HGRN_WIDTH
CONF_KERNEL = 31
SC_WIDTH = D_MODEL
SC_KERNEL = 3
D_FF = 256 * math.ceil(8 * D_MODEL / 3 / 256)
EVEN_IN = 4 * HGRN_WIDTH + 2 * CONF_WIDTH
ALPHA = (2 * DEPTH) ** 0.25
LN_EPS = 1e-5
RMS_EPS = 1e-6

SUBLANES = 8
LANES = 128
CHUNK = 128
N_LEVELS = CHUNK.bit_length() - 1
TB_EVEN = 512
TB_ODD = 512
TM_FFN = 512
CONF_HIST = 32
SC_HIST = SUBLANES
ROW_PITCH = 2
SAMPLE_GROUP = 8
VMEM_LIMIT = 56 * 1024 * 1024


def _sigmoid(x):
    return 1.0 / (1.0 + jnp.exp(-x))


def _silu(x):
    return x * _sigmoid(x)


def _layernorm(x, w, b):
    mu = jnp.mean(x, axis=-1, keepdims=True)
    xc = x - mu
    var = jnp.mean(xc * xc, axis=-1, keepdims=True)
    return xc * lax.rsqrt(var + LN_EPS) * w + b


def _dot(a, b):
    return jnp.dot(a, b, preferred_element_type=F32)


def _dot_nt(a, b):
    return lax.dot_general(a, b, (((1,), (1,)), ((), ())), preferred_element_type=F32)


def _dot_tn(a, b):
    return lax.dot_general(a, b, (((0,), (0,)), ((), ())), preferred_element_type=F32)


def _split3(x):
    hi = x.astype(BF16)
    r1 = x - hi.astype(F32)
    mid = r1.astype(BF16)
    lo = (r1 - mid.astype(F32)).astype(BF16)
    return hi, mid, lo


def _lower_bound(logits, e):
    m = jnp.max(logits, axis=0, keepdims=True)
    ex = jnp.exp(logits - m)
    sm = ex / jnp.sum(ex, axis=0, keepdims=True)
    lb = jnp.zeros_like(m)
    for i in range(1, e + 1):
        lb = lb + sm[i:i + 1]
    return lb


def _hgrn_gates(zq, zf, lb):
    t = jnp.exp(-jnp.abs(zf))
    r = 1.0 / (1.0 + t)
    sig_neg = jnp.where(zf >= 0.0, t * r, r)
    log_sig = jnp.minimum(zf, 0.0) - jnp.log1p(t)
    a = jnp.log(lb)
    c = jnp.log1p(-lb) + log_sig
    logf = jnp.maximum(a, c) + jnp.log1p(jnp.exp(-jnp.abs(a - c)))
    kk = (1.0 - lb) * sig_neg
    qq = _silu(zq) * (HGRN_DK ** -0.5)
    return qq, kk, logf


def _level_operand(q, k, e_j, j):
    L, width = q.shape
    half = 1 << (j - 1)
    if half >= SUBLANES:
        pieces = []
        for s in range(0, L, 2 * half):
            pieces.append(k[s:s + half])
            pieces.append(q[s + half:s + 2 * half])
        base = jnp.concatenate(pieces, axis=0)
    else:
        sub = lax.broadcasted_iota(jnp.int32, (1, SUBLANES, width), 1)
        upper = ((sub >> (j - 1)) & 1) == 1
        shape3 = (L // SUBLANES, SUBLANES, width)
        base = jnp.where(upper, q.reshape(shape3), k.reshape(shape3)).reshape(L, width)
    return (base * e_j).astype(BF16)


def _block_diag(a):
    z = jnp.zeros((a.shape[0], LANES), a.dtype)
    return jnp.concatenate([jnp.concatenate([a[:, :LANES], z], axis=1),
                            jnp.concatenate([z, a[:, LANES:]], axis=1)], axis=0)


def _chunk_constants():
    L = CHUNK
    t = np.arange(L)[:, None]
    r = np.arange(L)[None, :]
    mats = []
    for j in range(1, N_LEVELS + 1):
        half = 1 << (j - 1)
        start = (t >> j) << j
        mid = start + half - 1
        upper = t >= start + half
        d = np.where(upper, (r > mid) & (r <= t), (r > t) & (r <= mid))
        mats.append(d)
    mats.append(r <= t)
    mats.append(r > t)
    d_all = np.concatenate(mats, axis=0).astype(np.float32)
    x = np.bitwise_xor(t, r)
    lvl = np.zeros((L, L), np.int32)
    nz = x > 0
    lvl[nz] = np.floor(np.log2(x[nz])).astype(np.int32) + 1
    lvl = np.where(r > t, -1, lvl).astype(np.int32)
    d_all = np.concatenate([d_all] * 3, axis=1)
    lvl = np.concatenate([lvl, lvl], axis=1)
    return jnp.asarray(d_all, dtype=BF16), jnp.asarray(lvl)


def _pitched(start, n):
    return pl.ds(ROW_PITCH * start, n, stride=ROW_PITCH)


def _zero_history(buf_ref, n_hist):
    for c in range(buf_ref.shape[0]):
        buf_ref[c, _pitched(0, n_hist), :] = jnp.zeros((n_hist, LANES), F32)


def _causal_dwconv(buf_ref, new_rows, w_ref, n_hist):
    tb, width = new_rows.shape
    taps = w_ref.shape[0]
    off = n_hist - (taps - 1)
    outs, hists = [], []
    for c in range(width // LANES):
        cols = slice(c * LANES, (c + 1) * LANES)
        buf_ref[c, _pitched(n_hist, tb), :] = new_rows[:, cols]
        acc = w_ref[taps - 1:taps, cols] * new_rows[:, cols]
        for j in range(taps - 1):
            acc = acc + w_ref[j:j + 1, cols] * buf_ref[c, _pitched(off + j, tb), :]
        outs.append(acc)
        hist = buf_ref[c, _pitched(tb, n_hist), :]
        buf_ref[c, _pitched(0, n_hist), :] = hist
        hists.append(hist)
    return jnp.concatenate(outs, axis=-1), jnp.concatenate(hists, axis=-1)


def _even_prompt_kernel(e, n_t, x_ref, w_in_ref, w_out_ref, logit_ref, gn_ref, dww_ref, dwb_ref, clw_ref, clb_ref,
                        lnw_ref, lnb_ref, dall_ref, lvl_ref, y_ref, s_out_ref, conf_out_ref, st_ref, ubuf_ref):
    t_idx = pl.program_id(1)
    tb = x_ref.shape[0]
    hw = HGRN_WIDTH

    @pl.when(t_idx == 0)
    def _():
        st_ref[...] = jnp.zeros_like(st_ref)
        _zero_history(ubuf_ref, CONF_HIST)

    x = x_ref[...]
    p = _dot(x.astype(BF16), w_in_ref[...])
    zq = p[:, 0:hw]
    zf = p[:, hw:2 * hw]
    vi = p[:, 2 * hw:3 * hw]
    zg = p[:, 3 * hw:4 * hw]
    ca = p[:, 4 * hw:4 * hw + CONF_WIDTH]
    cg = p[:, 4 * hw + CONF_WIDTH:]

    lb = _lower_bound(logit_ref[...], e)
    qq, kk, logf = _hgrn_gates(zq, zf, lb)
    lvl = lvl_ref[...]
    dall = dall_ref[...]
    gn = gn_ref[...]
    L = CHUNK
    o_chunks = []
    for c in range(tb // L):
        rows = slice(c * L, (c + 1) * L)
        g3 = jnp.concatenate(_split3(logf[rows]), axis=0)
        ee = jnp.exp(_dot(dall, g3))
        o_heads = []
        for hp in range(HGRN_HEADS // 2):
            cols = slice(2 * hp * HGRN_DK, 2 * (hp + 1) * HGRN_DK)
            q2 = qq[rows, cols]
            k2 = kk[rows, cols]
            v2 = vi[rows, cols].astype(BF16)
            scores = jnp.where(lvl == 0, _dot_nt(q2.astype(BF16), _block_diag(k2.astype(BF16))), 0.0)
            for j in range(1, N_LEVELS + 1):
                m_j = _level_operand(q2, k2, ee[(j - 1) * L:j * L, cols], j)
                scores = jnp.where(lvl == j, _dot_nt(m_j, _block_diag(m_j)), scores)
            e_q = ee[N_LEVELS * L:(N_LEVELS + 1) * L, cols]
            e_k = ee[(N_LEVELS + 1) * L:(N_LEVELS + 2) * L, cols]
            qd = (q2 * e_q).astype(BF16)
            kd = (k2 * e_k).astype(BF16)
            st = [st_ref[2 * hp], st_ref[2 * hp + 1]]
            st2 = jnp.concatenate([st[0].astype(BF16), st[1].astype(BF16)], axis=1)
            o2 = _dot(scores.astype(BF16), _block_diag(v2)) + _dot_nt(qd, _block_diag(st2))
            for i in range(2):
                hc = slice(i * HGRN_DK, (i + 1) * HGRN_DK)
                st_ref[2 * hp + i] = e_q[L - 1:L, hc] * st[i] + _dot_tn(v2[:, hc], kd[:, hc])
                o_h = o2[:, hc]
                o_heads.append(o_h * lax.rsqrt(jnp.mean(o_h * o_h, axis=-1, keepdims=True) + RMS_EPS) * gn)
        o_chunks.append(jnp.concatenate(o_heads, axis=-1))
    o = jnp.concatenate(o_chunks, axis=0) * _silu(zg)

    conv, hist = _causal_dwconv(ubuf_ref, ca * _sigmoid(cg), dww_ref, CONF_HIST)
    cc = _silu(_layernorm(conv + dwb_ref[...], clw_ref[...], clb_ref[...]))

    @pl.when(t_idx == n_t - 1)
    def _():
        conf_out_ref[...] = hist[CONF_HIST - (CONF_KERNEL - 1):, :]
        for h in range(HGRN_HEADS):
            s_out_ref[h] = st_ref[h].T

    y = _dot(jnp.concatenate([o, cc], axis=-1).astype(BF16), w_out_ref[...])
    y_ref[...] = _layernorm(ALPHA * x + y, lnw_ref[...], lnb_ref[...])


def _even_prompt(x, e, layer, w_in, w_out, logits, gn, dww, dwb, clw, clb, lnw, lnb, dall, lvl):
    bsz, t_len, d = x.shape
    tb = min(TB_EVEN, t_len)
    assert t_len % tb == 0 and tb % CHUNK == 0 and t_len >= CONF_KERNEL - 1
    n_t = t_len // tb
    const2 = lambda b, t: (0, 0)
    return pl.pallas_call(
        functools.partial(_even_prompt_kernel, e, n_t),
        grid=(bsz, n_t),
        in_specs=[
            pl.BlockSpec((None, tb, d), lambda b, t: (b, t, 0)),
            pl.BlockSpec((None, d, EVEN_IN), lambda b, t: (e, 0, 0)),
            pl.BlockSpec((None, d, d), lambda b, t: (e, 0, 0)),
            pl.BlockSpec(logits.shape, const2),
            pl.BlockSpec((None, 1, HGRN_DV), lambda b, t: (e, 0, 0)),
            pl.BlockSpec((None, CONF_KERNEL, CONF_WIDTH), lambda b, t: (e, 0, 0)),
            pl.BlockSpec((None, 1, CONF_WIDTH), lambda b, t: (e, 0, 0)),
            pl.BlockSpec((None, 1, CONF_WIDTH), lambda b, t: (e, 0, 0)),
            pl.BlockSpec((None, 1, CONF_WIDTH), lambda b, t: (e, 0, 0)),
            pl.BlockSpec((None, 1, d), lambda b, t: (layer, 0, 0)),
            pl.BlockSpec((None, 1, d), lambda b, t: (layer, 0, 0)),
            pl.BlockSpec(dall.shape, const2),
            pl.BlockSpec(lvl.shape, const2),
        ],
        out_specs=[
            pl.BlockSpec((None, tb, d), lambda b, t: (b, t, 0)),
            pl.BlockSpec((None, HGRN_HEADS, HGRN_DK, HGRN_DV), lambda b, t: (b, 0, 0, 0)),
            pl.BlockSpec((None, CONF_KERNEL - 1, CONF_WIDTH), lambda b, t: (b, 0, 0)),
        ],
        out_shape=[
            jax.ShapeDtypeStruct((bsz, t_len, d), F32),
            jax.ShapeDtypeStruct((bsz, HGRN_HEADS, HGRN_DK, HGRN_DV), F32),
            jax.ShapeDtypeStruct((bsz, CONF_KERNEL - 1, CONF_WIDTH), F32),
        ],
        scratch_shapes=[
            pltpu.VMEM((HGRN_HEADS, HGRN_DV, HGRN_DK), F32),
            pltpu.VMEM((CONF_WIDTH // LANES, ROW_PITCH * (CONF_HIST + tb), LANES), F32),
        ],
        compiler_params=pltpu.CompilerParams(dimension_semantics=("arbitrary", "arbitrary"),
                                             vmem_limit_bytes=VMEM_LIMIT),
        name=f"even_prompt_{e}",
    )(x, w_in, w_out, logits, gn, dww, dwb, clw, clb, lnw, lnb, dall, lvl)


def _even_sample_kernel(e, n_g, aliased, x_ref, w_in_ref, w_out_ref, logit_ref, gn_ref, dww_ref, dwb_ref, clw_ref,
                        clb_ref, lnw_ref, lnb_ref, s_ref, c_ref, *rest):
    if aliased:
        rest = rest[2:]
    y_ref, s_out_ref, c_out_ref, qt_ref, at_ref, kt_ref, v_ref, gate_ref, u_ref, o_ref, cacc_ref = rest
    i = pl.program_id(0)
    hw = HGRN_WIDTH
    n_seq = x_ref.shape[0]

    @pl.when(i == 0)
    def _():
        p = _dot(x_ref[...].astype(BF16), w_in_ref[...])
        lb = _lower_bound(logit_ref[...], e)
        qq, kk, logf = _hgrn_gates(p[:, 0:hw], p[:, hw:2 * hw], lb)
        qt_ref[...] = qq.T
        kt_ref[...] = kk.T
        at_ref[...] = jnp.exp(logf).T
        v_ref[...] = p[:, 2 * hw:3 * hw]
        gate_ref[...] = _silu(p[:, 3 * hw:4 * hw])
        u_ref[...] = p[:, 4 * hw:4 * hw + CONF_WIDTH] * _sigmoid(p[:, 4 * hw + CONF_WIDTH:])

    lane = lax.broadcasted_iota(jnp.int32, (HGRN_DK, n_seq), 1)
    n_hist = CONF_KERNEL - 1
    w_hist = dww_ref[0:n_hist, :]
    w_last = dww_ref[n_hist:n_hist + 1, :]
    grp = pl.ds(pl.multiple_of(i * SAMPLE_GROUP, SAMPLE_GROUP), SAMPLE_GROUP)
    v_grp = v_ref[grp, :]
    u_grp = u_ref[grp, :]
    o_rows, c_rows = [], []
    for j in range(SAMPLE_GROUP):
        pick = lane == i * SAMPLE_GROUP + j
        o_heads = []
        for h in range(HGRN_HEADS):
            cols = slice(h * HGRN_DK, (h + 1) * HGRN_DK)
            a_col = jnp.sum(jnp.where(pick, at_ref[cols, :], 0.0), axis=1, keepdims=True)
            k_col = jnp.sum(jnp.where(pick, kt_ref[cols, :], 0.0), axis=1, keepdims=True)
            q_col = jnp.sum(jnp.where(pick, qt_ref[cols, :], 0.0), axis=1, keepdims=True)
            s_new = a_col * s_ref[j, h] + k_col * v_grp[j:j + 1, cols]
            s_out_ref[j, h] = s_new
            o_heads.append(jnp.sum(q_col * s_new, axis=0, keepdims=True))
        o_rows.append(jnp.concatenate(o_heads, axis=-1))
        buf = c_ref[j]
        u_row = u_grp[j:j + 1, :]
        c_rows.append(jnp.sum(buf * w_hist, axis=0, keepdims=True) + w_last * u_row + dwb_ref[...])
        c_out_ref[j, 0:n_hist - 1, :] = buf[1:n_hist, :]
        c_out_ref[j, n_hist - 1:n_hist, :] = u_row
    o_ref[grp, :] = jnp.concatenate(o_rows, axis=0)
    cacc_ref[grp, :] = jnp.concatenate(c_rows, axis=0)

    @pl.when(i == n_g - 1)
    def _():
        gn = gn_ref[...]
        o_heads = []
        for h in range(HGRN_HEADS):
            o_h = o_ref[:, h * HGRN_DK:(h + 1) * HGRN_DK]
            o_heads.append(o_h * lax.rsqrt(jnp.mean(o_h * o_h, axis=-1, keepdims=True) + RMS_EPS) * gn)
        o = jnp.concatenate(o_heads, axis=-1) * gate_ref[...]
        cc = _silu(_layernorm(cacc_ref[...], clw_ref[...], clb_ref[...]))
        y = _dot(jnp.concatenate([o, cc], axis=-1).astype(BF16), w_out_ref[...])
        y_ref[...] = _layernorm(ALPHA * x_ref[...] + y, lnw_ref[...], lnb_ref[...])


def _even_sample(x, e, layer, w_in, w_out, logits, gn, dww, dwb, clw, clb, lnw, lnb, state_hgrn, state_conf,
                 prev_h=None, prev_c=None):
    n_seq, d = x.shape
    g = SAMPLE_GROUP
    assert n_seq % g == 0 and n_seq % 128 == 0
    n_g = n_seq // g
    aliased = prev_h is not None
    const2 = lambda i: (0, 0)
    in_specs = [
        pl.BlockSpec((n_seq, d), const2),
        pl.BlockSpec((None, d, EVEN_IN), lambda i: (e, 0, 0)),
        pl.BlockSpec((None, d, d), lambda i: (e, 0, 0)),
        pl.BlockSpec(logits.shape, const2),
        pl.BlockSpec((None, 1, HGRN_DV), lambda i: (e, 0, 0)),
        pl.BlockSpec((None, CONF_KERNEL, CONF_WIDTH), lambda i: (e, 0, 0)),
        pl.BlockSpec((None, 1, CONF_WIDTH), lambda i: (e, 0, 0)),
        pl.BlockSpec((None, 1, CONF_WIDTH), lambda i: (e, 0, 0)),
        pl.BlockSpec((None, 1, CONF_WIDTH), lambda i: (e, 0, 0)),
        pl.BlockSpec((None, 1, d), lambda i: (layer, 0, 0)),
        pl.BlockSpec((None, 1, d), lambda i: (layer, 0, 0)),
        pl.BlockSpec((None, g, HGRN_HEADS, HGRN_DK, HGRN_DV), lambda i: (e, i, 0, 0, 0)),
        pl.BlockSpec((None, g, CONF_KERNEL - 1, CONF_WIDTH), lambda i: (e, i, 0, 0)),
    ]
    args = [x, w_in, w_out, logits, gn, dww, dwb, clw, clb, lnw, lnb, state_hgrn, state_conf]
    aliases = {}
    if aliased:
        in_specs += [pl.BlockSpec(memory_space=pl.ANY), pl.BlockSpec(memory_space=pl.ANY)]
        aliases = {len(args): 1, len(args) + 1: 2}
        args += [prev_h, prev_c]
    return pl.pallas_call(
        functools.partial(_even_sample_kernel, e, n_g, aliased),
        grid=(n_g,),
        in_specs=in_specs,
        out_specs=[
            pl.BlockSpec((n_seq, d), const2),
            pl.BlockSpec((None, g, HGRN_HEADS, HGRN_DK, HGRN_DV), lambda i: (e, i, 0, 0, 0)),
            pl.BlockSpec((None, g, CONF_KERNEL - 1, CONF_WIDTH), lambda i: (e, i, 0, 0)),
        ],
        out_shape=[
            jax.ShapeDtypeStruct((n_seq, d), F32),
            jax.ShapeDtypeStruct(state_hgrn.shape, F32),
            jax.ShapeDtypeStruct(state_conf.shape, F32),
        ],
        scratch_shapes=[
            pltpu.VMEM((HGRN_WIDTH, n_seq), F32),
            pltpu.VMEM((HGRN_WIDTH, n_seq), F32),
            pltpu.VMEM((HGRN_WIDTH, n_seq), F32),
            pltpu.VMEM((n_seq, HGRN_WIDTH), F32),
            pltpu.VMEM((n_seq, HGRN_WIDTH), F32),
            pltpu.VMEM((n_seq, CONF_WIDTH), F32),
            pltpu.VMEM((n_seq, HGRN_WIDTH), F32),
            pltpu.VMEM((n_seq, CONF_WIDTH), F32),
        ],
        input_output_aliases=aliases,
        compiler_params=pltpu.CompilerParams(dimension_semantics=("arbitrary",), vmem_limit_bytes=VMEM_LIMIT),
        name=f"even_sample_{e}",
    )(*args)


def _odd_prompt_kernel(n_t, x_ref, w_in_ref, w_out_ref, cw_ref, lnw_ref, lnb_ref, y_ref, z_out_ref, zbuf_ref):
    t_idx = pl.program_id(1)
    tb = x_ref.shape[0]
    w = SC_WIDTH

    @pl.when(t_idx == 0)
    def _():
        _zero_history(zbuf_ref, SC_HIST)

    x = x_ref[...]
    p = _dot(x.astype(BF16), w_in_ref[...])
    bg = p[:, 0:w]
    conv, hist = _causal_dwconv(zbuf_ref, p[:, w:2 * w] * p[:, 2 * w:3 * w], cw_ref, SC_HIST)

    @pl.when(t_idx == n_t - 1)
    def _():
        z_out_ref[...] = hist[SC_HIST - (SC_KERNEL - 1):, :]

    y = _dot((bg * conv).astype(BF16), w_out_ref[...])
    y_ref[...] = _layernorm(ALPHA * x + y, lnw_ref[...], lnb_ref[...])


def _odd_prompt(x, o, layer, w_in, w_out, cw, lnw, lnb):
    bsz, t_len, d = x.shape
    tb = min(TB_ODD, t_len)
    assert t_len % tb == 0 and t_len >= SC_KERNEL - 1
    n_t = t_len // tb
    return pl.pallas_call(
        functools.partial(_odd_prompt_kernel, n_t),
        grid=(bsz, n_t),
        in_specs=[
            pl.BlockSpec((None, tb, d), lambda b, t: (b, t, 0)),
            pl.BlockSpec((None, d, 3 * SC_WIDTH), lambda b, t: (o, 0, 0)),
            pl.BlockSpec((None, SC_WIDTH, d), lambda b, t: (o, 0, 0)),
            pl.BlockSpec((None, SC_KERNEL, SC_WIDTH), lambda b, t: (o, 0, 0)),
            pl.BlockSpec((None, 1, d), lambda b, t: (layer, 0, 0)),
            pl.BlockSpec((None, 1, d), lambda b, t: (layer, 0, 0)),
        ],
        out_specs=[
            pl.BlockSpec((None, tb, d), lambda b, t: (b, t, 0)),
            pl.BlockSpec((None, SC_KERNEL - 1, SC_WIDTH), lambda b, t: (b, 0, 0)),
        ],
        out_shape=[
            jax.ShapeDtypeStruct((bsz, t_len, d), F32),
            jax.ShapeDtypeStruct((bsz, SC_KERNEL - 1, SC_WIDTH), F32),
        ],
        scratch_shapes=[pltpu.VMEM((SC_WIDTH // LANES, ROW_PITCH * (SC_HIST + tb), LANES), F32)],
        compiler_params=pltpu.CompilerParams(dimension_semantics=("arbitrary", "arbitrary"),
                                             vmem_limit_bytes=VMEM_LIMIT),
        name=f"odd_prompt_{o}",
    )(x, w_in, w_out, cw, lnw, lnb)


def _odd_sample_kernel(x_ref, w_in_ref, w_out_ref, cw_ref, lnw_ref, lnb_ref, s_ref, y_ref, s_out_ref):
    w = SC_WIDTH
    x = x_ref[...]
    p = _dot(x.astype(BF16), w_in_ref[...])
    z = p[:, w:2 * w] * p[:, 2 * w:3 * w]
    s0 = s_ref[:, 0:w]
    s1 = s_ref[:, w:2 * w]
    conv = cw_ref[0:1, :] * s0 + cw_ref[1:2, :] * s1 + cw_ref[2:3, :] * z
    s_out_ref[:, 0:w] = s1
    s_out_ref[:, w:2 * w] = z
    y = _dot((p[:, 0:w] * conv).astype(BF16), w_out_ref[...])
    y_ref[...] = _layernorm(ALPHA * x + y, lnw_ref[...], lnb_ref[...])


def _odd_sample(x, o, layer, w_in, w_out, cw, lnw, lnb, state):
    n_seq, d = x.shape
    assert SC_KERNEL == 3
    hist = (SC_KERNEL - 1) * SC_WIDTH
    return pl.pallas_call(
        _odd_sample_kernel,
        grid=(1,),
        in_specs=[
            pl.BlockSpec((n_seq, d), lambda i: (0, 0)),
            pl.BlockSpec((None, d, 3 * SC_WIDTH), lambda i: (o, 0, 0)),
            pl.BlockSpec((None, SC_WIDTH, d), lambda i: (o, 0, 0)),
            pl.BlockSpec((None, SC_KERNEL, SC_WIDTH), lambda i: (o, 0, 0)),
            pl.BlockSpec((None, 1, d), lambda i: (layer, 0, 0)),
            pl.BlockSpec((None, 1, d), lambda i: (layer, 0, 0)),
            pl.BlockSpec((None, n_seq, hist), lambda i: (o, 0, 0)),
        ],
        out_specs=[
            pl.BlockSpec((n_seq, d), lambda i: (0, 0)),
            pl.BlockSpec((n_seq, hist), lambda i: (0, 0)),
        ],
        out_shape=[
            jax.ShapeDtypeStruct((n_seq, d), F32),
            jax.ShapeDtypeStruct((n_seq, hist), F32),
        ],
        compiler_params=pltpu.CompilerParams(dimension_semantics=("arbitrary",), vmem_limit_bytes=VMEM_LIMIT),
        name=f"odd_sample_{o}",
    )(x, w_in, w_out, cw, lnw, lnb, state)


def _ffn_kernel(x_ref, w1_ref, w3_ref, w2_ref, lnw_ref, lnb_ref, y_ref):
    x = x_ref[...]
    xb = x.astype(BF16)
    h1 = _dot(xb, w1_ref[...])
    h3 = _dot(xb, w3_ref[...])
    y = _dot((_silu(h1) * h3).astype(BF16), w2_ref[...])
    y_ref[...] = _layernorm(ALPHA * x + y, lnw_ref[...], lnb_ref[...])


def _ffn(x, layer, w1, w3, w2, lnw, lnb):
    m, d = x.shape
    tm = min(TM_FFN, m)
    assert m % tm == 0
    return pl.pallas_call(
        _ffn_kernel,
        grid=(m // tm,),
        in_specs=[
            pl.BlockSpec((tm, d), lambda i: (i, 0)),
            pl.BlockSpec((None, d, D_FF), lambda i: (layer, 0, 0)),
            pl.BlockSpec((None, d, D_FF), lambda i: (layer, 0, 0)),
            pl.BlockSpec((None, D_FF, d), lambda i: (layer, 0, 0)),
            pl.BlockSpec((None, 1, d), lambda i: (layer, 0, 0)),
            pl.BlockSpec((None, 1, d), lambda i: (layer, 0, 0)),
        ],
        out_specs=pl.BlockSpec((tm, d), lambda i: (i, 0)),
        out_shape=jax.ShapeDtypeStruct((m, d), F32),
        compiler_params=pltpu.CompilerParams(dimension_semantics=("arbitrary",), vmem_limit_bytes=VMEM_LIMIT),
        name=f"ffn_{layer}",
    )(x, w1, w3, w2, lnw, lnb)


def kernel(x_prompt, x_sample, state_hgrn, state_conf, state_sconv, w_in_even, w_out_even, hgrn_lb_logits,
           hgrn_gnorm_w, conf_dw_w, conf_dw_b, conf_ln_w, conf_ln_b, sc_w_in, sc_conv_w, sc_w_out,
           ffn_w1, ffn_w3, ffn_w2, ln_mix_w, ln_mix_b, ln_ffn_w, ln_ffn_b):
    bsz, t_len, d = x_prompt.shape
    n_seq = x_sample.shape[0]
    n_even, n_odd = state_hgrn.shape[0], state_sconv.shape[0]
    assert d == D_MODEL and x_sample.shape[1] == 1

    w_in_even = w_in_even.astype(BF16)
    w_out_even = w_out_even.astype(BF16)
    sc_w_in = sc_w_in.astype(BF16)
    sc_w_out = sc_w_out.astype(BF16)
    ffn_w1 = ffn_w1.astype(BF16)
    ffn_w3 = ffn_w3.astype(BF16)
    ffn_w2 = ffn_w2.astype(BF16)
    row = lambda a: a.reshape(a.shape[0], 1, a.shape[1])
    gn, dwb, clw, clb = row(hgrn_gnorm_w), row(conf_dw_b), row(conf_ln_w), row(conf_ln_b)
    lmw, lmb, lfw, lfb = row(ln_mix_w), row(ln_mix_b), row(ln_ffn_w), row(ln_ffn_b)
    sconv_flat = state_sconv.reshape(n_odd, n_seq, (SC_KERNEL - 1) * SC_WIDTH)
    dall, lvl = _chunk_constants()

    xp = x_prompt
    xs = x_sample.reshape(n_seq, d)
    h_p, c_p, s_p, s_s = [], [], [], []
    h_s = c_s = None
    for layer in range(DEPTH):
        if layer % 2 == 0:
            e = layer // 2
            xp, sh, cb = _even_prompt(xp, e, layer, w_in_even, w_out_even, hgrn_lb_logits, gn, conf_dw_w, dwb,
                                      clw, clb, lmw, lmb, dall, lvl)
            h_p.append(sh)
            c_p.append(cb)
            xs, h_s, c_s = _even_sample(xs, e, layer, w_in_even, w_out_even, hgrn_lb_logits, gn, conf_dw_w, dwb,
                                        clw, clb, lmw, lmb, state_hgrn, state_conf, h_s, c_s)
        else:
            o = layer // 2
            xp, sb = _odd_prompt(xp, o, layer, sc_w_in, sc_w_out, sc_conv_w, lmw, lmb)
            s_p.append(sb)
            xs, sb = _odd_sample(xs, o, layer, sc_w_in, sc_w_out, sc_conv_w, lmw, lmb, sconv_flat)
            s_s.append(sb.reshape(n_seq, SC_KERNEL - 1, SC_WIDTH))
        xp = _ffn(xp.reshape(bsz * t_len, d), layer, ffn_w1, ffn_w3, ffn_w2, lfw, lfb).reshape(bsz, t_len, d)
        xs = _ffn(xs, layer, ffn_w1, ffn_w3, ffn_w2, lfw, lfb)
    return (xp, xs.reshape(n_seq, 1, d), jnp.stack(h_p), jnp.stack(c_p), jnp.stack(s_p), h_s, c_s,
            jnp.stack(s_s))
```

```python
import functools
import math

import numpy as np
import jax
import jax.numpy as jnp
from jax import lax
from jax.experimental import pallas as pl
from jax.experimental.pallas import tpu as pltpu

F32 = jnp.float32
BF16 = jnp.bfloat16

D_MODEL = 1024
DEPTH = 4
HGRN_WIDTH = D_MODEL // 2
HGRN_DK = 128
HGRN_DV = 128
HGRN_HEADS = HGRN_WIDTH // HGRN_DK
CONF_WIDTH = D_MODEL - HGRN_WIDTH
CONF_KERNEL = 31
SC_WIDTH = D_MODEL
SC_KERNEL = 3
D_FF = 256 * math.ceil(8 * D_MODEL / 3 / 256)
EVEN_IN = 4 * HGRN_WIDTH + 2 * CONF_WIDTH
ALPHA = (2 * DEPTH) ** 0.25
LN_EPS = 1e-5
RMS_EPS = 1e-6

SUBLANES = 8
LANES = 128
CHUNK = 128
N_LEVELS = CHUNK.bit_length() - 1
N_MXU_LEVELS = SUBLANES.bit_length() - 1
TB_EVEN = 512
TB_ODD = 512
TM_FFN = 512
CONF_HIST = 32
SC_HIST = SUBLANES
ROW_PITCH = 2
SAMPLE_GROUP = 8
VMEM_LIMIT = 56 * 1024 * 1024


def _sigmoid(x):
    return 1.0 / (1.0 + jnp.exp(-x))


def _silu(x):
    return x * _sigmoid(x)


def _layernorm(x, w, b):
    mu = jnp.mean(x, axis=-1, keepdims=True)
    xc = x - mu
    var = jnp.mean(xc * xc, axis=-1, keepdims=True)
    return xc * lax.rsqrt(var + LN_EPS) * w + b


def _dot(a, b):
    return jnp.dot(a, b, preferred_element_type=F32)


def _dot_nt(a, b):
    return lax.dot_general(a, b, (((1,), (1,)), ((), ())), preferred_element_type=F32)


def _dot_tn(a, b):
    return lax.dot_general(a, b, (((0,), (0,)), ((), ())), preferred_element_type=F32)


def _split3(x):
    hi = x.astype(BF16)
    r1 = x - hi.astype(F32)
    mid = r1.astype(BF16)
    lo = (r1 - mid.astype(F32)).astype(BF16)
    return hi, mid, lo


def _lower_bound(logits, e):
    m = jnp.max(logits, axis=0, keepdims=True)
    ex = jnp.exp(logits - m)
    sm = ex / jnp.sum(ex, axis=0, keepdims=True)
    lb = jnp.zeros_like(m)
    for i in range(1, e + 1):
        lb = lb + sm[i:i + 1]
    return lb


def _hgrn_gates(zq, zf, lb):
    t = jnp.exp(-jnp.abs(zf))
    s = 1.0 + t
    r = 1.0 / s
    sig_neg = jnp.where(zf >= 0.0, t * r, r)
    log_sig = jnp.minimum(zf, 0.0) - jnp.log(s)
    a = jnp.log(lb)
    c = jnp.log1p(-lb) + log_sig
    logf = jnp.maximum(a, c) + jnp.log(1.0 + jnp.exp(-jnp.abs(a - c)))
    kk = (1.0 - lb) * sig_neg
    qq = _silu(zq) * (HGRN_DK ** -0.5)
    return qq, kk, logf


def _level_operand(q, k, e_j, j):
    L, width = q.shape
    half = 1 << (j - 1)
    if half >= SUBLANES:
        pieces = []
        for s in range(0, L, 2 * half):
            pieces.append(k[s:s + half])
            pieces.append(q[s + half:s + 2 * half])
        base = jnp.concatenate(pieces, axis=0)
    else:
        sub = lax.broadcasted_iota(jnp.int32, (1, SUBLANES, width), 1)
        upper = ((sub >> (j - 1)) & 1) == 1
        shape3 = (L // SUBLANES, SUBLANES, width)
        base = jnp.where(upper, q.reshape(shape3), k.reshape(shape3)).reshape(L, width)
    return (base * e_j).astype(BF16)


def _block_diag(a):
    z = jnp.zeros((a.shape[0], LANES), a.dtype)
    return jnp.concatenate([jnp.concatenate([a[:, :LANES], z], axis=1),
                            jnp.concatenate([z, a[:, LANES:]], axis=1)], axis=0)


def _chunk_constants():
    L = CHUNK
    t = np.arange(L)[:, None]
    r = np.arange(L)[None, :]
    mats = []
    for j in range(1, N_MXU_LEVELS + 1):
        half = 1 << (j - 1)
        start = (t >> j) << j
        mid = start + half - 1
        upper = t >= start + half
        d = np.where(upper, (r > mid) & (r <= t), (r > t) & (r <= mid))
        mats.append(d)
    mats.append(r <= t)
    d_all = np.concatenate(mats, axis=0).astype(np.float32)
    x = np.bitwise_xor(t, r)
    lvl = np.zeros((L, L), np.int32)
    nz = x > 0
    lvl[nz] = np.floor(np.log2(x[nz])).astype(np.int32) + 1
    lvl = np.where(r > t, -1, lvl).astype(np.int32)
    d_all = np.concatenate([d_all] * 3, axis=1)
    lvl = np.concatenate([lvl, lvl], axis=1)
    return jnp.asarray(d_all, dtype=BF16), jnp.asarray(lvl)


def _pitched(start, n):
    return pl.ds(ROW_PITCH * start, n, stride=ROW_PITCH)


def _zero_history(buf_ref, n_hist):
    for c in range(buf_ref.shape[0]):
        buf_ref[c, _pitched(0, n_hist), :] = jnp.zeros((n_hist, LANES), F32)


def _causal_dwconv(buf_ref, new_rows, w_ref, n_hist):
    tiles = [_causal_dwconv_tile(buf_ref, c, new_rows, w_ref, n_hist) for c in range(new_rows.shape[1] // LANES)]
    return jnp.concatenate([t[0] for t in tiles], axis=-1), jnp.concatenate([t[1] for t in tiles], axis=-1)


def _causal_dwconv_tile(buf_ref, c, new_rows, w_ref, n_hist):
    tb = new_rows.shape[0]
    taps = w_ref.shape[0]
    off = n_hist - (taps - 1)
    cols = slice(c * LANES, (c + 1) * LANES)
    buf_ref[c, _pitched(n_hist, tb), :] = new_rows[:, cols]
    acc = w_ref[taps - 1:taps, cols] * new_rows[:, cols]
    for j in range(taps - 1):
        acc = acc + w_ref[j:j + 1, cols] * buf_ref[c, _pitched(off + j, tb), :]
    hist = buf_ref[c, _pitched(tb, n_hist), :]
    buf_ref[c, _pitched(0, n_hist), :] = hist
    return acc, hist


def _even_prompt_kernel(e, n_t, x_ref, w_in_ref, w_out_ref, logit_ref, gn_ref, dww_ref, dwb_ref, clw_ref, clb_ref,
                        lnw_ref, lnb_ref, dall_ref, lvl_ref, y_ref, s_out_ref, conf_out_ref, st_ref, ubuf_ref):
    t_idx = pl.program_id(1)
    tb = x_ref.shape[0]
    hw = HGRN_WIDTH

    @pl.when(t_idx == 0)
    def _():
        st_ref[...] = jnp.zeros_like(st_ref)
        _zero_history(ubuf_ref, CONF_HIST)

    x = x_ref[...]
    xb = x.astype(BF16)
    L = CHUNK
    n_tiles = CONF_WIDTH // LANES

    pc = _dot(xb, w_in_ref[:, 4 * hw:])
    u = pc[:, :CONF_WIDTH] * _sigmoid(pc[:, CONF_WIDTH:])
    conv_tiles = [_causal_dwconv_tile(ubuf_ref, 0, u, dww_ref, CONF_HIST)]
    pq = _dot(xb, w_in_ref[:, 0:2 * hw])
    conv_tiles.append(_causal_dwconv_tile(ubuf_ref, 1, u, dww_ref, CONF_HIST))
    pv = _dot(xb, w_in_ref[:, 2 * hw:4 * hw])
    vi = pv[:, 0:hw]
    zg = pv[:, hw:]

    def conv_finish():
        conv = jnp.concatenate([t[0] for t in conv_tiles], axis=-1)
        cc = _silu(_layernorm(conv + dwb_ref[...], clw_ref[...], clb_ref[...]))
        return _dot(cc.astype(BF16), w_out_ref[hw:, :])

    fill = [lambda c=c: conv_tiles.append(_causal_dwconv_tile(ubuf_ref, c, u, dww_ref, CONF_HIST))
            for c in range(2, n_tiles)] + [conv_finish]
    fill_out = []

    lb = _lower_bound(logit_ref[...], e)
    lvl = lvl_ref[...]
    dall = dall_ref[...]
    gn = gn_ref[...]
    o_chunks = []
    for c in range(tb // L):
        rows = slice(c * L, (c + 1) * L)
        qq, kk, logf = _hgrn_gates(pq[rows, 0:hw], pq[rows, hw:], lb)
        g3 = jnp.concatenate(_split3(logf), axis=0)
        eb = _dot(dall, g3)
        b = eb[N_MXU_LEVELS * L:, :]
        e_lvl = [jnp.exp(eb[(j - 1) * L:j * L, :]) for j in range(1, N_MXU_LEVELS + 1)]
        for j in range(N_MXU_LEVELS + 1, N_LEVELS + 1):
            size, half = 1 << j, 1 << (j - 1)
            b_mid = jnp.concatenate([jnp.broadcast_to(b[s + half - 1:s + half, :], (size, hw))
                                     for s in range(0, L, size)], axis=0)
            e_lvl.append(jnp.exp(-jnp.abs(b - b_mid)))
        e_q_all = jnp.exp(b)
        e_k_all = jnp.exp(b[L - 1:L, :] - b)
        o_heads = []
        for hp in range(HGRN_HEADS // 2):
            cols = slice(2 * hp * HGRN_DK, 2 * (hp + 1) * HGRN_DK)
            q2 = qq[:, cols]
            k2 = kk[:, cols]
            v2 = vi[rows, cols].astype(BF16)
            scores = jnp.where(lvl == 0, _dot_nt(q2.astype(BF16), _block_diag(k2.astype(BF16))), 0.0)
            for j in range(1, N_LEVELS + 1):
                m_j = _level_operand(q2, k2, e_lvl[j - 1][:, cols], j)
                scores = jnp.where(lvl == j, _dot_nt(m_j, _block_diag(m_j)), scores)
            e_q = e_q_all[:, cols]
            e_k = e_k_all[:, cols]
            qd = (q2 * e_q).astype(BF16)
            kd = (k2 * e_k).astype(BF16)
            st = [st_ref[2 * hp], st_ref[2 * hp + 1]]
            st2 = jnp.concatenate([st[0].astype(BF16), st[1].astype(BF16)], axis=1)
            o2 = _dot(scores.astype(BF16), _block_diag(v2)) + _dot_nt(qd, _block_diag(st2))
            for i in range(2):
                hc = slice(i * HGRN_DK, (i + 1) * HGRN_DK)
                st_ref[2 * hp + i] = e_q[L - 1:L, hc] * st[i] + _dot_tn(v2[:, hc], kd[:, hc])
                o_h = o2[:, hc]
                o_heads.append(o_h * lax.rsqrt(jnp.mean(o_h * o_h, axis=-1, keepdims=True) + RMS_EPS) * gn)
        o_chunks.append(jnp.concatenate(o_heads, axis=-1) * _silu(zg[rows]))
        if fill:
            fill_out.append(fill.pop(0)())
    while fill:
        fill_out.append(fill.pop(0)())
    y = fill_out[-1]
    o = jnp.concatenate(o_chunks, axis=0)

    @pl.when(t_idx == n_t - 1)
    def _():
        hist = jnp.concatenate([t[1] for t in conv_tiles], axis=-1)
        conf_out_ref[...] = hist[CONF_HIST - (CONF_KERNEL - 1):, :]
        for h in range(HGRN_HEADS):
            s_out_ref[h] = st_ref[h].T

    y = y + _dot(o.astype(BF16), w_out_ref[0:hw, :])
    y_ref[...] = _layernorm(ALPHA * x + y, lnw_ref[...], lnb_ref[...])


def _even_prompt(x, e, layer, w_in, w_out, logits, gn, dww, dwb, clw, clb, lnw, lnb, dall, lvl):
    bsz, t_len, d = x.shape
    tb = min(TB_EVEN, t_len)
    assert t_len % tb == 0 and tb % CHUNK == 0 and t_len >= CONF_KERNEL - 1
    n_t = t_len // tb
    const2 = lambda b, t: (0, 0)
    return pl.pallas_call(
        functools.partial(_even_prompt_kernel, e, n_t),
        grid=(bsz, n_t),
        in_specs=[
            pl.BlockSpec((None, tb, d), lambda b, t: (b, t, 0)),
            pl.BlockSpec((None, d, EVEN_IN), lambda b, t: (e, 0, 0)),
            pl.BlockSpec((None, d, d), lambda b, t: (e, 0, 0)),
            pl.BlockSpec(logits.shape, const2),
            pl.BlockSpec((None, 1, HGRN_DV), lambda b, t: (e, 0, 0)),
            pl.BlockSpec((None, CONF_KERNEL, CONF_WIDTH), lambda b, t: (e, 0, 0)),
            pl.BlockSpec((None, 1, CONF_WIDTH), lambda b, t: (e, 0, 0)),
            pl.BlockSpec((None, 1, CONF_WIDTH), lambda b, t: (e, 0, 0)),
            pl.BlockSpec((None, 1, CONF_WIDTH), lambda b, t: (e, 0, 0)),
            pl.BlockSpec((None, 1, d), lambda b, t: (layer, 0, 0)),
            pl.BlockSpec((None, 1, d), lambda b, t: (layer, 0, 0)),
            pl.BlockSpec(dall.shape, const2),
            pl.BlockSpec(lvl.shape, const2),
        ],
        out_specs=[
            pl.BlockSpec((None, tb, d), lambda b, t: (b, t, 0)),
            pl.BlockSpec((None, HGRN_HEADS, HGRN_DK, HGRN_DV), lambda b, t: (b, 0, 0, 0)),
            pl.BlockSpec((None, CONF_KERNEL - 1, CONF_WIDTH), lambda b, t: (b, 0, 0)),
        ],
        out_shape=[
            jax.ShapeDtypeStruct((bsz, t_len, d), F32),
            jax.ShapeDtypeStruct((bsz, HGRN_HEADS, HGRN_DK, HGRN_DV), F32),
            jax.ShapeDtypeStruct((bsz, CONF_KERNEL - 1, CONF_WIDTH), F32),
        ],
        scratch_shapes=[
            pltpu.VMEM((HGRN_HEADS, HGRN_DV, HGRN_DK), F32),
            pltpu.VMEM((CONF_WIDTH // LANES, ROW_PITCH * (CONF_HIST + tb), LANES), F32),
        ],
        compiler_params=pltpu.CompilerParams(dimension_semantics=("arbitrary", "arbitrary"),
                                             vmem_limit_bytes=VMEM_LIMIT),
        name=f"even_prompt_{e}",
    )(x, w_in, w_out, logits, gn, dww, dwb, clw, clb, lnw, lnb, dall, lvl)


def _even_sample_kernel(e, n_g, aliased, x_ref, w_in_ref, w_out_ref, logit_ref, gn_ref, dww_ref, dwb_ref, clw_ref,
                        clb_ref, lnw_ref, lnb_ref, s_ref, c_ref, *rest):
    if aliased:
        rest = rest[2:]
    y_ref, s_out_ref, c_out_ref, qt_ref, at_ref, kt_ref, v_ref, gate_ref, u_ref, o_ref, cacc_ref = rest
    i = pl.program_id(0)
    hw = HGRN_WIDTH
    n_seq = x_ref.shape[0]

    @pl.when(i == 0)
    def _():
        p = _dot(x_ref[...].astype(BF16), w_in_ref[...])
        lb = _lower_bound(logit_ref[...], e)
        qq, kk, logf = _hgrn_gates(p[:, 0:hw], p[:, hw:2 * hw], lb)
        qt_ref[...] = qq.T
        kt_ref[...] = kk.T
        at_ref[...] = jnp.exp(logf).T
        v_ref[...] = p[:, 2 * hw:3 * hw]
        gate_ref[...] = _silu(p[:, 3 * hw:4 * hw])
        u_ref[...] = p[:, 4 * hw:4 * hw + CONF_WIDTH] * _sigmoid(p[:, 4 * hw + CONF_WIDTH:])

    lane = lax.broadcasted_iota(jnp.int32, (HGRN_DK, n_seq), 1)
    n_hist = CONF_KERNEL - 1
    w_hist = dww_ref[0:n_hist, :]
    w_last = dww_ref[n_hist:n_hist + 1, :]
    grp = pl.ds(pl.multiple_of(i * SAMPLE_GROUP, SAMPLE_GROUP), SAMPLE_GROUP)
    v_grp = v_ref[grp, :]
    u_grp = u_ref[grp, :]
    o_rows, c_rows = [], []
    for j in range(SAMPLE_GROUP):
        pick = lane == i * SAMPLE_GROUP + j
        o_heads = []
        for h in range(HGRN_HEADS):
            cols = slice(h * HGRN_DK, (h + 1) * HGRN_DK)
            a_col = jnp.sum(jnp.where(pick, at_ref[cols, :], 0.0), axis=1, keepdims=True)
            k_col = jnp.sum(jnp.where(pick, kt_ref[cols, :], 0.0), axis=1, keepdims=True)
            q_col = jnp.sum(jnp.where(pick, qt_ref[cols, :], 0.0), axis=1, keepdims=True)
            s_new = a_col * s_ref[j, h] + k_col * v_grp[j:j + 1, cols]
            s_out_ref[j, h] = s_new
            o_heads.append(jnp.sum(q_col * s_new, axis=0, keepdims=True))
        o_rows.append(jnp.concatenate(o_heads, axis=-1))
        buf = c_ref[j]
        u_row = u_grp[j:j + 1, :]
        c_rows.append(jnp.sum(buf * w_hist, axis=0, keepdims=True) + w_last * u_row + dwb_ref[...])
        c_out_ref[j, 0:n_hist - 1, :] = buf[1:n_hist, :]
        c_out_ref[j, n_hist - 1:n_hist, :] = u_row
    o_ref[grp, :] = jnp.concatenate(o_rows, axis=0)
    cacc_ref[grp, :] = jnp.concatenate(c_rows, axis=0)

    @pl.when(i == n_g - 1)
    def _():
        gn = gn_ref[...]
        o_heads = []
        for h in range(HGRN_HEADS):
            o_h = o_ref[:, h * HGRN_DK:(h + 1) * HGRN_DK]
            o_heads.append(o_h * lax.rsqrt(jnp.mean(o_h * o_h, axis=-1, keepdims=True) + RMS_EPS) * gn)
        o = jnp.concatenate(o_heads, axis=-1) * gate_ref[...]
        cc = _silu(_layernorm(cacc_ref[...], clw_ref[...], clb_ref[...]))
        y = _dot(jnp.concatenate([o, cc], axis=-1).astype(BF16), w_out_ref[...])
        y_ref[...] = _layernorm(ALPHA * x_ref[...] + y, lnw_ref[...], lnb_ref[...])


def _even_sample(x, e, layer, w_in, w_out, logits, gn, dww, dwb, clw, clb, lnw, lnb, state_hgrn, state_conf,
                 prev_h=None, prev_c=None):
    n_seq, d = x.shape
    g = SAMPLE_GROUP
    assert n_seq % g == 0 and n_seq % 128 == 0
    n_g = n_seq // g
    aliased = prev_h is not None
    const2 = lambda i: (0, 0)
    in_specs = [
        pl.BlockSpec((n_seq, d), const2),
        pl.BlockSpec((None, d, EVEN_IN), lambda i: (e, 0, 0)),
        pl.BlockSpec((None, d, d), lambda i: (e, 0, 0)),
        pl.BlockSpec(logits.shape, const2),
        pl.BlockSpec((None, 1, HGRN_DV), lambda i: (e, 0, 0)),
        pl.BlockSpec((None, CONF_KERNEL, CONF_WIDTH), lambda i: (e, 0, 0)),
        pl.BlockSpec((None, 1, CONF_WIDTH), lambda i: (e, 0, 0)),
        pl.BlockSpec((None, 1, CONF_WIDTH), lambda i: (e, 0, 0)),
        pl.BlockSpec((None, 1, CONF_WIDTH), lambda i: (e, 0, 0)),
        pl.BlockSpec((None, 1, d), lambda i: (layer, 0, 0)),
        pl.BlockSpec((None, 1, d), lambda i: (layer, 0, 0)),
        pl.BlockSpec((None, g, HGRN_HEADS, HGRN_DK, HGRN_DV), lambda i: (e, i, 0, 0, 0)),
        pl.BlockSpec((None, g, CONF_KERNEL - 1, CONF_WIDTH), lambda i: (e, i, 0, 0)),
    ]
    args = [x, w_in, w_out, logits, gn, dww, dwb, clw, clb, lnw, lnb, state_hgrn, state_conf]
    aliases = {}
    if aliased:
        in_specs += [pl.BlockSpec(memory_space=pl.ANY), pl.BlockSpec(memory_space=pl.ANY)]
        aliases = {len(args): 1, len(args) + 1: 2}
        args += [prev_h, prev_c]
    return pl.pallas_call(
        functools.partial(_even_sample_kernel, e, n_g, aliased),
        grid=(n_g,),
        in_specs=in_specs,
        out_specs=[
            pl.BlockSpec((n_seq, d), const2),
            pl.BlockSpec((None, g, HGRN_HEADS, HGRN_DK, HGRN_DV), lambda i: (e, i, 0, 0, 0)),
            pl.BlockSpec((None, g, CONF_KERNEL - 1, CONF_WIDTH), lambda i: (e, i, 0, 0)),
        ],
        out_shape=[
            jax.ShapeDtypeStruct((n_seq, d), F32),
            jax.ShapeDtypeStruct(state_hgrn.shape, F32),
            jax.ShapeDtypeStruct(state_conf.shape, F32),
        ],
        scratch_shapes=[
            pltpu.VMEM((HGRN_WIDTH, n_seq), F32),
            pltpu.VMEM((HGRN_WIDTH, n_seq), F32),
            pltpu.VMEM((HGRN_WIDTH, n_seq), F32),
            pltpu.VMEM((n_seq, HGRN_WIDTH), F32),
            pltpu.VMEM((n_seq, HGRN_WIDTH), F32),
            pltpu.VMEM((n_seq, CONF_WIDTH), F32),
            pltpu.VMEM((n_seq, HGRN_WIDTH), F32),
            pltpu.VMEM((n_seq, CONF_WIDTH), F32),
        ],
        input_output_aliases=aliases,
        compiler_params=pltpu.CompilerParams(dimension_semantics=("arbitrary",), vmem_limit_bytes=VMEM_LIMIT),
        name=f"even_sample_{e}",
    )(*args)


def _odd_prompt_kernel(n_t, x_ref, w_in_ref, w_out_ref, cw_ref, lnw_ref, lnb_ref, y_ref, z_out_ref, zbuf_ref):
    t_idx = pl.program_id(1)
    tb = x_ref.shape[0]
    w = SC_WIDTH

    @pl.when(t_idx == 0)
    def _():
        _zero_history(zbuf_ref, SC_HIST)

    x = x_ref[...]
    p = _dot(x.astype(BF16), w_in_ref[...])
    bg = p[:, 0:w]
    conv, hist = _causal_dwconv(zbuf_ref, p[:, w:2 * w] * p[:, 2 * w:3 * w], cw_ref, SC_HIST)

    @pl.when(t_idx == n_t - 1)
    def _():
        z_out_ref[...] = hist[SC_HIST - (SC_KERNEL - 1):, :]

    y = _dot((bg * conv).astype(BF16), w_out_ref[...])
    y_ref[...] = _layernorm(ALPHA * x + y, lnw_ref[...], lnb_ref[...])


def _odd_prompt(x, o, layer, w_in, w_out, cw, lnw, lnb):
    bsz, t_len, d = x.shape
    tb = min(TB_ODD, t_len)
    assert t_len % tb == 0 and t_len >= SC_KERNEL - 1
    n_t = t_len // tb
    return pl.pallas_call(
        functools.partial(_odd_prompt_kernel, n_t),
        grid=(bsz, n_t),
        in_specs=[
            pl.BlockSpec((None, tb, d), lambda b, t: (b, t, 0)),
            pl.BlockSpec((None, d, 3 * SC_WIDTH), lambda b, t: (o, 0, 0)),
            pl.BlockSpec((None, SC_WIDTH, d), lambda b, t: (o, 0, 0)),
            pl.BlockSpec((None, SC_KERNEL, SC_WIDTH), lambda b, t: (o, 0, 0)),
            pl.BlockSpec((None, 1, d), lambda b, t: (layer, 0, 0)),
            pl.BlockSpec((None, 1, d), lambda b, t: (layer, 0, 0)),
        ],
        out_specs=[
            pl.BlockSpec((None, tb, d), lambda b, t: (b, t, 0)),
            pl.BlockSpec((None, SC_KERNEL - 1, SC_WIDTH), lambda b, t: (b, 0, 0)),
        ],
        out_shape=[
            jax.ShapeDtypeStruct((bsz, t_len, d), F32),
            jax.ShapeDtypeStruct((bsz, SC_KERNEL - 1, SC_WIDTH), F32),
        ],
        scratch_shapes=[pltpu.VMEM((SC_WIDTH // LANES, ROW_PITCH * (SC_HIST + tb), LANES), F32)],
        compiler_params=pltpu.CompilerParams(dimension_semantics=("arbitrary", "arbitrary"),
                                             vmem_limit_bytes=VMEM_LIMIT),
        name=f"odd_prompt_{o}",
    )(x, w_in, w_out, cw, lnw, lnb)


def _odd_sample_kernel(x_ref, w_in_ref, w_out_ref, cw_ref, lnw_ref, lnb_ref, s_ref, y_ref, s_out_ref):
    w = SC_WIDTH
    x = x_ref[...]
    p = _dot(x.astype(BF16), w_in_ref[...])
    z = p[:, w:2 * w] * p[:, 2 * w:3 * w]
    s0 = s_ref[:, 0:w]
    s1 = s_ref[:, w:2 * w]
    conv = cw_ref[0:1, :] * s0 + cw_ref[1:2, :] * s1 + cw_ref[2:3, :] * z
    s_out_ref[:, 0:w] = s1
    s_out_ref[:, w:2 * w] = z
    y = _dot((p[:, 0:w] * conv).astype(BF16), w_out_ref[...])
    y_ref[...] = _layernorm(ALPHA * x + y, lnw_ref[...], lnb_ref[...])


def _odd_sample(x, o, layer, w_in, w_out, cw, lnw, lnb, state):
    n_seq, d = x.shape
    assert SC_KERNEL == 3
    hist = (SC_KERNEL - 1) * SC_WIDTH
    return pl.pallas_call(
        _odd_sample_kernel,
        grid=(1,),
        in_specs=[
            pl.BlockSpec((n_seq, d), lambda i: (0, 0)),
            pl.BlockSpec((None, d, 3 * SC_WIDTH), lambda i: (o, 0, 0)),
            pl.BlockSpec((None, SC_WIDTH, d), lambda i: (o, 0, 0)),
            pl.BlockSpec((None, SC_KERNEL, SC_WIDTH), lambda i: (o, 0, 0)),
            pl.BlockSpec((None, 1, d), lambda i: (layer, 0, 0)),
            pl.BlockSpec((None, 1, d), lambda i: (layer, 0, 0)),
            pl.BlockSpec((None, n_seq, hist), lambda i: (o, 0, 0)),
        ],
        out_specs=[
            pl.BlockSpec((n_seq, d), lambda i: (0, 0)),
            pl.BlockSpec((n_seq, hist), lambda i: (0, 0)),
        ],
        out_shape=[
            jax.ShapeDtypeStruct((n_seq, d), F32),
            jax.ShapeDtypeStruct((n_seq, hist), F32),
        ],
        compiler_params=pltpu.CompilerParams(dimension_semantics=("arbitrary",), vmem_limit_bytes=VMEM_LIMIT),
        name=f"odd_sample_{o}",
    )(x, w_in, w_out, cw, lnw, lnb, state)


def _ffn_kernel(n_p, xp_ref, xs_ref, w1_ref, w3_ref, w2_ref, lnw_ref, lnb_ref, yp_ref, ys_ref):
    def rows(x_ref, y_ref):
        x = x_ref[...]
        xb = x.astype(BF16)
        h1 = _dot(xb, w1_ref[...])
        h3 = _dot(xb, w3_ref[...])
        y = _dot((_silu(h1) * h3).astype(BF16), w2_ref[...])
        y_ref[...] = _layernorm(ALPHA * x + y, lnw_ref[...], lnb_ref[...])

    pl.when(pl.program_id(0) < n_p)(lambda: rows(xp_ref, yp_ref))
    pl.when(pl.program_id(0) == n_p)(lambda: rows(xs_ref, ys_ref))


def _ffn(xp, xs, layer, w1, w3, w2, lnw, lnb):
    m, d = xp.shape
    tm = min(TM_FFN, m)
    assert m % tm == 0
    n_p = m // tm
    prompt_block = pl.BlockSpec((tm, d), lambda i: (jnp.minimum(i, n_p - 1), 0))
    sample_block = pl.BlockSpec(xs.shape, lambda i: (0, 0))
    return pl.pallas_call(
        functools.partial(_ffn_kernel, n_p),
        grid=(n_p + 1,),
        in_specs=[
            prompt_block,
            sample_block,
            pl.BlockSpec((None, d, D_FF), lambda i: (layer, 0, 0)),
            pl.BlockSpec((None, d, D_FF), lambda i: (layer, 0, 0)),
            pl.BlockSpec((None, D_FF, d), lambda i: (layer, 0, 0)),
            pl.BlockSpec((None, 1, d), lambda i: (layer, 0, 0)),
            pl.BlockSpec((None, 1, d), lambda i: (layer, 0, 0)),
        ],
        out_specs=[prompt_block, sample_block],
        out_shape=[jax.ShapeDtypeStruct((m, d), F32), jax.ShapeDtypeStruct(xs.shape, F32)],
        compiler_params=pltpu.CompilerParams(dimension_semantics=("arbitrary",), vmem_limit_bytes=VMEM_LIMIT),
        name=f"ffn_{layer}",
    )(xp, xs, w1, w3, w2, lnw, lnb)


def kernel(x_prompt, x_sample, state_hgrn, state_conf, state_sconv, w_in_even, w_out_even, hgrn_lb_logits,
           hgrn_gnorm_w, conf_dw_w, conf_dw_b, conf_ln_w, conf_ln_b, sc_w_in, sc_conv_w, sc_w_out,
           ffn_w1, ffn_w3, ffn_w2, ln_mix_w, ln_mix_b, ln_ffn_w, ln_ffn_b):
    bsz, t_len, d = x_prompt.shape
    n_seq = x_sample.shape[0]
    n_even, n_odd = state_hgrn.shape[0], state_sconv.shape[0]
    assert d == D_MODEL and x_sample.shape[1] == 1

    w_in_even = w_in_even.astype(BF16)
    w_out_even = w_out_even.astype(BF16)
    sc_w_in = sc_w_in.astype(BF16)
    sc_w_out = sc_w_out.astype(BF16)
    ffn_w1 = ffn_w1.astype(BF16)
    ffn_w3 = ffn_w3.astype(BF16)
    ffn_w2 = ffn_w2.astype(BF16)
    row = lambda a: a.reshape(a.shape[0], 1, a.shape[1])
    gn, dwb, clw, clb = row(hgrn_gnorm_w), row(conf_dw_b), row(conf_ln_w), row(conf_ln_b)
    lmw, lmb, lfw, lfb = row(ln_mix_w), row(ln_mix_b), row(ln_ffn_w), row(ln_ffn_b)
    sconv_flat = state_sconv.reshape(n_odd, n_seq, (SC_KERNEL - 1) * SC_WIDTH)
    dall, lvl = _chunk_constants()

    xp = x_prompt
    xs = x_sample.reshape(n_seq, d)
    h_p, c_p, s_p, s_s = [], [], [], []
    h_s = c_s = None
    for layer in range(DEPTH):
        if layer % 2 == 0:
            e = layer // 2
            xp, sh, cb = _even_prompt(xp, e, layer, w_in_even, w_out_even, hgrn_lb_logits, gn, conf_dw_w, dwb,
                                      clw, clb, lmw, lmb, dall, lvl)
            h_p.append(sh)
            c_p.append(cb)
            xs, h_s, c_s = _even_sample(xs, e, layer, w_in_even, w_out_even, hgrn_lb_logits, gn, conf_dw_w, dwb,
                                        clw, clb, lmw, lmb, state_hgrn, state_conf, h_s, c_s)
        else:
            o = layer // 2
            xp, sb = _odd_prompt(xp, o, layer, sc_w_in, sc_w_out, sc_conv_w, lmw, lmb)
            s_p.append(sb)
            xs, sb = _odd_sample(xs, o, layer, sc_w_in, sc_w_out, sc_conv_w, lmw, lmb, sconv_flat)
            s_s.append(sb.reshape(n_seq, SC_KERNEL - 1, SC_WIDTH))
        xp, xs = _ffn(xp.reshape(bsz * t_len, d), xs, layer, ffn_w1, ffn_w3, ffn_w2, lfw, lfb)
        xp = xp.reshape(bsz, t_len, d)
    return (xp, xs.reshape(n_seq, 1, d), jnp.stack(h_p), jnp.stack(c_p), jnp.stack(s_p), h_s, c_s,
            jnp.stack(s_s))
```

```python
import functools
import math

import numpy as np
import jax
import jax.numpy as jnp
from jax import lax
from jax.experimental import pallas as pl
from jax.experimental.pallas import tpu as pltpu

F32 = jnp.float32
BF16 = jnp.bfloat16

D_MODEL = 1024
DEPTH = 4
HGRN_WIDTH = D_MODEL // 2
HGRN_DK = 128
HGRN_DV = 128
HGRN_HEADS = HGRN_WIDTH // HGRN_DK
CONF_WIDTH = D_MODEL - HGRN_WIDTH
CONF_KERNEL = 31
SC_WIDTH = D_MODEL
SC_KERNEL = 3
D_FF = 256 * math.ceil(8 * D_MODEL / 3 / 256)
EVEN_IN = 4 * HGRN_WIDTH + 2 * CONF_WIDTH
ALPHA = (2 * DEPTH) ** 0.25
LN_EPS = 1e-5
RMS_EPS = 1e-6

SUBLANES = 8
LANES = 128
CHUNK = 128
N_LEVELS = CHUNK.bit_length() - 1
N_MXU_LEVELS = SUBLANES.bit_length() - 1
TB_EVEN = 512
TB_ODD = 512
TM_FFN = 512
CONF_HIST = 32
SC_HIST = SUBLANES
ROW_PITCH = 2
SAMPLE_GROUP = 8
VMEM_LIMIT = 56 * 1024 * 1024


def _sigmoid(x):
    return 1.0 / (1.0 + jnp.exp(-x))


def _silu(x):
    return x * _sigmoid(x)


def _layernorm(x, w, b):
    mu = jnp.mean(x, axis=-1, keepdims=True)
    xc = x - mu
    var = jnp.mean(xc * xc, axis=-1, keepdims=True)
    return xc * lax.rsqrt(var + LN_EPS) * w + b


def _dot(a, b):
    return jnp.dot(a, b, preferred_element_type=F32)


def _dot_nt(a, b):
    return lax.dot_general(a, b, (((1,), (1,)), ((), ())), preferred_element_type=F32)


def _dot_tn(a, b):
    return lax.dot_general(a, b, (((0,), (0,)), ((), ())), preferred_element_type=F32)


def _split3(x):
    hi = x.astype(BF16)
    r1 = x - hi.astype(F32)
    mid = r1.astype(BF16)
    lo = (r1 - mid.astype(F32)).astype(BF16)
    return hi, mid, lo


def _lower_bound(logits, e):
    m = jnp.max(logits, axis=0, keepdims=True)
    ex = jnp.exp(logits - m)
    sm = ex / jnp.sum(ex, axis=0, keepdims=True)
    lb = jnp.zeros_like(m)
    for i in range(1, e + 1):
        lb = lb + sm[i:i + 1]
    return lb


def _hgrn_gates(zq, zf, lb):
    t = jnp.exp(-jnp.abs(zf))
    s = 1.0 + t
    r = 1.0 / s
    sig_neg = jnp.where(zf >= 0.0, t * r, r)
    log_sig = jnp.minimum(zf, 0.0) - jnp.log(s)
    a = jnp.log(lb)
    c = jnp.log1p(-lb) + log_sig
    logf = jnp.maximum(a, c) + jnp.log(1.0 + jnp.exp(-jnp.abs(a - c)))
    kk = (1.0 - lb) * sig_neg
    qq = _silu(zq) * (HGRN_DK ** -0.5)
    return qq, kk, logf


def _level_operand(q, k, e_j, j):
    L, width = q.shape
    half = 1 << (j - 1)
    if half >= SUBLANES:
        pieces = []
        for s in range(0, L, 2 * half):
            pieces.append(k[s:s + half])
            pieces.append(q[s + half:s + 2 * half])
        base = jnp.concatenate(pieces, axis=0)
    else:
        sub = lax.broadcasted_iota(jnp.int32, (1, SUBLANES, width), 1)
        upper = ((sub >> (j - 1)) & 1) == 1
        shape3 = (L // SUBLANES, SUBLANES, width)
        base = jnp.where(upper, q.reshape(shape3), k.reshape(shape3)).reshape(L, width)
    return (base * e_j).astype(BF16)


def _block_diag(a):
    z = jnp.zeros((a.shape[0], LANES), a.dtype)
    return jnp.concatenate([jnp.concatenate([a[:, :LANES], z], axis=1),
                            jnp.concatenate([z, a[:, LANES:]], axis=1)], axis=0)


def _chunk_constants():
    L = CHUNK
    t = np.arange(L)[:, None]
    r = np.arange(L)[None, :]
    mats = []
    for j in range(1, N_MXU_LEVELS + 1):
        half = 1 << (j - 1)
        start = (t >> j) << j
        mid = start + half - 1
        upper = t >= start + half
        d = np.where(upper, (r > mid) & (r <= t), (r > t) & (r <= mid))
        mats.append(d)
    mats.append(r <= t)
    d_all = np.concatenate(mats, axis=0).astype(np.float32)
    x = np.bitwise_xor(t, r)
    lvl = np.zeros((L, L), np.int32)
    nz = x > 0
    lvl[nz] = np.floor(np.log2(x[nz])).astype(np.int32) + 1
    lvl = np.where(r > t, -1, lvl).astype(np.int32)
    d_all = np.concatenate([d_all] * 3, axis=1)
    lvl = np.concatenate([lvl, lvl], axis=1)
    return jnp.asarray(d_all, dtype=BF16), jnp.asarray(lvl)


def _pitched(start, n):
    return pl.ds(ROW_PITCH * start, n, stride=ROW_PITCH)


def _zero_history(buf_ref, n_hist):
    for c in range(buf_ref.shape[0]):
        buf_ref[c, _pitched(0, n_hist), :] = jnp.zeros((n_hist, LANES), F32)


def _causal_dwconv(buf_ref, new_rows, w_ref, n_hist):
    tiles = [_causal_dwconv_tile(buf_ref, c, new_rows, w_ref, n_hist) for c in range(new_rows.shape[1] // LANES)]
    return jnp.concatenate([t[0] for t in tiles], axis=-1), jnp.concatenate([t[1] for t in tiles], axis=-1)


def _causal_dwconv_tile(buf_ref, c, new_rows, w_ref, n_hist):
    tb = new_rows.shape[0]
    taps = w_ref.shape[0]
    off = n_hist - (taps - 1)
    cols = slice(c * LANES, (c + 1) * LANES)
    buf_ref[c, _pitched(n_hist, tb), :] = new_rows[:, cols]
    acc = w_ref[taps - 1:taps, cols] * new_rows[:, cols]
    for j in range(taps - 1):
        acc = acc + w_ref[j:j + 1, cols] * buf_ref[c, _pitched(off + j, tb), :]
    hist = buf_ref[c, _pitched(tb, n_hist), :]
    buf_ref[c, _pitched(0, n_hist), :] = hist
    return acc, hist


def _even_prompt_kernel(e, n_t, x_ref, w_in_ref, w_out_ref, logit_ref, gn_ref, dww_ref, dwb_ref, clw_ref, clb_ref,
                        lnw_ref, lnb_ref, dall_ref, lvl_ref, y_ref, s_out_ref, conf_out_ref, st_ref, ubuf_ref):
    t_idx = pl.program_id(1)
    tb = x_ref.shape[0]
    hw = HGRN_WIDTH

    @pl.when(t_idx == 0)
    def _():
        st_ref[...] = jnp.zeros_like(st_ref)
        _zero_history(ubuf_ref, CONF_HIST)

    x = x_ref[...]
    xb = x.astype(BF16)
    L = CHUNK
    n_tiles = CONF_WIDTH // LANES

    pc = _dot(xb, w_in_ref[:, 4 * hw:])
    u = pc[:, :CONF_WIDTH] * _sigmoid(pc[:, CONF_WIDTH:])
    conv_tiles = [_causal_dwconv_tile(ubuf_ref, 0, u, dww_ref, CONF_HIST)]
    pq = _dot(xb, w_in_ref[:, 0:2 * hw])
    conv_tiles.append(_causal_dwconv_tile(ubuf_ref, 1, u, dww_ref, CONF_HIST))
    pv = _dot(xb, w_in_ref[:, 2 * hw:4 * hw])
    vi = pv[:, 0:hw]
    zg = pv[:, hw:]

    def conv_finish():
        conv = jnp.concatenate([t[0] for t in conv_tiles], axis=-1)
        cc = _silu(_layernorm(conv + dwb_ref[...], clw_ref[...], clb_ref[...]))
        return _dot(cc.astype(BF16), w_out_ref[hw:, :])

    fill = [lambda c=c: conv_tiles.append(_causal_dwconv_tile(ubuf_ref, c, u, dww_ref, CONF_HIST))
            for c in range(2, n_tiles)] + [conv_finish]
    fill_out = []

    lb = _lower_bound(logit_ref[...], e)
    lvl = lvl_ref[...]
    dall = dall_ref[...]
    gn = gn_ref[...]
    o_chunks = []
    for c in range(tb // L):
        rows = slice(c * L, (c + 1) * L)
        qq, kk, logf = _hgrn_gates(pq[rows, 0:hw], pq[rows, hw:], lb)
        g3 = jnp.concatenate(_split3(logf), axis=0)
        eb = _dot(dall, g3)
        b = eb[N_MXU_LEVELS * L:, :]
        e_lvl = [jnp.exp(eb[(j - 1) * L:j * L, :]) for j in range(1, N_MXU_LEVELS + 1)]
        for j in range(N_MXU_LEVELS + 1, N_LEVELS + 1):
            size, half = 1 << j, 1 << (j - 1)
            b_mid = jnp.concatenate([jnp.broadcast_to(b[s + half - 1:s + half, :], (size, hw))
                                     for s in range(0, L, size)], axis=0)
            e_lvl.append(jnp.exp(-jnp.abs(b - b_mid)))
        e_q_all = jnp.exp(b)
        e_k_all = jnp.exp(b[L - 1:L, :] - b)
        o_heads = []
        for hp in range(HGRN_HEADS // 2):
            cols = slice(2 * hp * HGRN_DK, 2 * (hp + 1) * HGRN_DK)
            q2 = qq[:, cols]
            k2 = kk[:, cols]
            v2 = vi[rows, cols].astype(BF16)
            scores = jnp.where(lvl == 0, _dot_nt(q2.astype(BF16), _block_diag(k2.astype(BF16))), 0.0)
            for j in range(1, N_LEVELS + 1):
                m_j = _level_operand(q2, k2, e_lvl[j - 1][:, cols], j)
                scores = jnp.where(lvl == j, _dot_nt(m_j, _block_diag(m_j)), scores)
            e_q = e_q_all[:, cols]
            e_k = e_k_all[:, cols]
            qd = (q2 * e_q).astype(BF16)
            kd = (k2 * e_k).astype(BF16)
            st = [st_ref[2 * hp], st_ref[2 * hp + 1]]
            st2 = jnp.concatenate([st[0].astype(BF16), st[1].astype(BF16)], axis=1)
            o2 = _dot(scores.astype(BF16), _block_diag(v2)) + _dot_nt(qd, _block_diag(st2))
            for i in range(2):
                hc = slice(i * HGRN_DK, (i + 1) * HGRN_DK)
                st_ref[2 * hp + i] = e_q[L - 1:L, hc] * st[i] + _dot_tn(v2[:, hc], kd[:, hc])
                o_h = o2[:, hc]
                o_heads.append(o_h * lax.rsqrt(jnp.mean(o_h * o_h, axis=-1, keepdims=True) + RMS_EPS) * gn)
        o_chunks.append(jnp.concatenate(o_heads, axis=-1) * _silu(zg[rows]))
        if fill:
            fill_out.append(fill.pop(0)())
    while fill:
        fill_out.append(fill.pop(0)())
    y = fill_out[-1]
    o = jnp.concatenate(o_chunks, axis=0)

    @pl.when(t_idx == n_t - 1)
    def _():
        hist = jnp.concatenate([t[1] for t in conv_tiles], axis=-1)
        conf_out_ref[...] = hist[CONF_HIST - (CONF_KERNEL - 1):, :]
        for h in range(HGRN_HEADS):
            s_out_ref[h] = st_ref[h].T

    y = y + _dot(o.astype(BF16), w_out_ref[0:hw, :])
    y_ref[...] = _layernorm(ALPHA * x + y, lnw_ref[...], lnb_ref[...])


def _even_prompt(x, e, layer, w_in, w_out, logits, gn, dww, dwb, clw, clb, lnw, lnb, dall, lvl):
    bsz, t_len, d = x.shape
    tb = min(TB_EVEN, t_len)
    assert t_len % tb == 0 and tb % CHUNK == 0 and t_len >= CONF_KERNEL - 1
    n_t = t_len // tb
    const2 = lambda b, t: (0, 0)
    return pl.pallas_call(
        functools.partial(_even_prompt_kernel, e, n_t),
        grid=(bsz, n_t),
        in_specs=[
            pl.BlockSpec((None, tb, d), lambda b, t: (b, t, 0)),
            pl.BlockSpec((None, d, EVEN_IN), lambda b, t: (e, 0, 0)),
            pl.BlockSpec((None, d, d), lambda b, t: (e, 0, 0)),
            pl.BlockSpec(logits.shape, const2),
            pl.BlockSpec((None, 1, HGRN_DV), lambda b, t: (e, 0, 0)),
            pl.BlockSpec((None, CONF_KERNEL, CONF_WIDTH), lambda b, t: (e, 0, 0)),
            pl.BlockSpec((None, 1, CONF_WIDTH), lambda b, t: (e, 0, 0)),
            pl.BlockSpec((None, 1, CONF_WIDTH), lambda b, t: (e, 0, 0)),
            pl.BlockSpec((None, 1, CONF_WIDTH), lambda b, t: (e, 0, 0)),
            pl.BlockSpec((None, 1, d), lambda b, t: (layer, 0, 0)),
            pl.BlockSpec((None, 1, d), lambda b, t: (layer, 0, 0)),
            pl.BlockSpec(dall.shape, const2),
            pl.BlockSpec(lvl.shape, const2),
        ],
        out_specs=[
            pl.BlockSpec((None, tb, d), lambda b, t: (b, t, 0)),
            pl.BlockSpec((None, HGRN_HEADS, HGRN_DK, HGRN_DV), lambda b, t: (b, 0, 0, 0)),
            pl.BlockSpec((None, CONF_KERNEL - 1, CONF_WIDTH), lambda b, t: (b, 0, 0)),
        ],
        out_shape=[
            jax.ShapeDtypeStruct((bsz, t_len, d), F32),
            jax.ShapeDtypeStruct((bsz, HGRN_HEADS, HGRN_DK, HGRN_DV), F32),
            jax.ShapeDtypeStruct((bsz, CONF_KERNEL - 1, CONF_WIDTH), F32),
        ],
        scratch_shapes=[
            pltpu.VMEM((HGRN_HEADS, HGRN_DV, HGRN_DK), F32),
            pltpu.VMEM((CONF_WIDTH // LANES, ROW_PITCH * (CONF_HIST + tb), LANES), F32),
        ],
        compiler_params=pltpu.CompilerParams(dimension_semantics=("arbitrary", "arbitrary"),
                                             vmem_limit_bytes=VMEM_LIMIT),
        name=f"even_prompt_{e}",
    )(x, w_in, w_out, logits, gn, dww, dwb, clw, clb, lnw, lnb, dall, lvl)


def _even_sample_kernel(e, n_g, aliased, x_ref, w_in_ref, w_out_ref, logit_ref, gn_ref, dww_ref, dwb_ref, clw_ref,
                        clb_ref, lnw_ref, lnb_ref, s_ref, c_ref, *rest):
    if aliased:
        rest = rest[2:]
    y_ref, s_out_ref, c_out_ref, qt_ref, at_ref, kt_ref, v_ref, gate_ref, u_ref, o_ref, cacc_ref = rest
    i = pl.program_id(0)
    hw = HGRN_WIDTH
    n_seq = x_ref.shape[0]

    @pl.when(i == 0)
    def _():
        p = _dot(x_ref[...].astype(BF16), w_in_ref[...])
        lb = _lower_bound(logit_ref[...], e)
        qq, kk, logf = _hgrn_gates(p[:, 0:hw], p[:, hw:2 * hw], lb)
        qt_ref[...] = qq.T
        kt_ref[...] = kk.T
        at_ref[...] = jnp.exp(logf).T
        v_ref[...] = p[:, 2 * hw:3 * hw]
        gate_ref[...] = _silu(p[:, 3 * hw:4 * hw])
        u_ref[...] = p[:, 4 * hw:4 * hw + CONF_WIDTH] * _sigmoid(p[:, 4 * hw + CONF_WIDTH:])

    lane = lax.broadcasted_iota(jnp.int32, (HGRN_DK, n_seq), 1)
    n_hist = CONF_KERNEL - 1
    grp = pl.ds(pl.multiple_of(i * SAMPLE_GROUP, SAMPLE_GROUP), SAMPLE_GROUP)
    v_grp = v_ref[grp, :]
    u_grp = u_ref[grp, :]

    conv = dww_ref[n_hist:n_hist + 1, :] * u_grp + dwb_ref[...]
    for j in range(n_hist):
        conv = conv + dww_ref[j:j + 1, :] * c_ref[j]
    cacc_ref[grp, :] = conv
    c_out_ref[0:n_hist - 1] = c_ref[1:n_hist]
    c_out_ref[n_hist - 1] = u_grp

    o_rows = []
    for j in range(SAMPLE_GROUP):
        pick = lane == i * SAMPLE_GROUP + j
        o_heads = []
        for h in range(HGRN_HEADS):
            cols = slice(h * HGRN_DK, (h + 1) * HGRN_DK)
            a_col = jnp.sum(jnp.where(pick, at_ref[cols, :], 0.0), axis=1, keepdims=True)
            k_col = jnp.sum(jnp.where(pick, kt_ref[cols, :], 0.0), axis=1, keepdims=True)
            q_col = jnp.sum(jnp.where(pick, qt_ref[cols, :], 0.0), axis=1, keepdims=True)
            s_new = a_col * s_ref[j, h] + k_col * v_grp[j:j + 1, cols]
            s_out_ref[j, h] = s_new
            o_heads.append(jnp.sum(q_col * s_new, axis=0, keepdims=True))
        o_rows.append(jnp.concatenate(o_heads, axis=-1))
    o_ref[grp, :] = jnp.concatenate(o_rows, axis=0)

    @pl.when(i == n_g - 1)
    def _():
        gn = gn_ref[...]
        o_heads = []
        for h in range(HGRN_HEADS):
            o_h = o_ref[:, h * HGRN_DK:(h + 1) * HGRN_DK]
            o_heads.append(o_h * lax.rsqrt(jnp.mean(o_h * o_h, axis=-1, keepdims=True) + RMS_EPS) * gn)
        o = jnp.concatenate(o_heads, axis=-1) * gate_ref[...]
        cc = _silu(_layernorm(cacc_ref[...], clw_ref[...], clb_ref[...]))
        y = _dot(jnp.concatenate([o, cc], axis=-1).astype(BF16), w_out_ref[...])
        y_ref[...] = _layernorm(ALPHA * x_ref[...] + y, lnw_ref[...], lnb_ref[...])


def _even_sample(x, e, layer, w_in, w_out, logits, gn, dww, dwb, clw, clb, lnw, lnb, state_hgrn, state_conf,
                 prev_h=None, prev_c=None):
    n_seq, d = x.shape
    g = SAMPLE_GROUP
    assert n_seq % g == 0 and n_seq % 128 == 0
    n_g = n_seq // g
    aliased = prev_h is not None
    const2 = lambda i: (0, 0)
    in_specs = [
        pl.BlockSpec((n_seq, d), const2),
        pl.BlockSpec((None, d, EVEN_IN), lambda i: (e, 0, 0)),
        pl.BlockSpec((None, d, d), lambda i: (e, 0, 0)),
        pl.BlockSpec(logits.shape, const2),
        pl.BlockSpec((None, 1, HGRN_DV), lambda i: (e, 0, 0)),
        pl.BlockSpec((None, CONF_KERNEL, CONF_WIDTH), lambda i: (e, 0, 0)),
        pl.BlockSpec((None, 1, CONF_WIDTH), lambda i: (e, 0, 0)),
        pl.BlockSpec((None, 1, CONF_WIDTH), lambda i: (e, 0, 0)),
        pl.BlockSpec((None, 1, CONF_WIDTH), lambda i: (e, 0, 0)),
        pl.BlockSpec((None, 1, d), lambda i: (layer, 0, 0)),
        pl.BlockSpec((None, 1, d), lambda i: (layer, 0, 0)),
        pl.BlockSpec((None, g, HGRN_HEADS, HGRN_DK, HGRN_DV), lambda i: (e, i, 0, 0, 0)),
        pl.BlockSpec((None, CONF_KERNEL - 1, g, CONF_WIDTH), lambda i: (e, 0, i, 0)),
    ]
    args =[x, w_in, w_out, logits, gn, dww, dwb, clw, clb, lnw, lnb, state_hgrn, state_conf]
    aliases = {}
    if aliased:
        in_specs += [pl.BlockSpec(memory_space=pl.ANY), pl.BlockSpec(memory_space=pl.ANY)]
        aliases = {len(args): 1, len(args) + 1: 2}
        args += [prev_h, prev_c]
    return pl.pallas_call(
        functools.partial(_even_sample_kernel, e, n_g, aliased),
        grid=(n_g,),
        in_specs=in_specs,
        out_specs=[
            pl.BlockSpec((n_seq, d), const2),
            pl.BlockSpec((None, g, HGRN_HEADS, HGRN_DK, HGRN_DV), lambda i: (e, i, 0, 0, 0)),
            pl.BlockSpec((None, CONF_KERNEL - 1, g, CONF_WIDTH), lambda i: (e, 0, i, 0)),
        ],
        out_shape=[
            jax.ShapeDtypeStruct((n_seq, d), F32),
            jax.ShapeDtypeStruct(state_hgrn.shape, F32),
            jax.ShapeDtypeStruct(state_conf.shape, F32),
        ],
        scratch_shapes=[
            pltpu.VMEM((HGRN_WIDTH, n_seq), F32),
            pltpu.VMEM((HGRN_WIDTH, n_seq), F32),
            pltpu.VMEM((HGRN_WIDTH, n_seq), F32),
            pltpu.VMEM((n_seq, HGRN_WIDTH), F32),
            pltpu.VMEM((n_seq, HGRN_WIDTH), F32),
            pltpu.VMEM((n_seq, CONF_WIDTH), F32),
            pltpu.VMEM((n_seq, HGRN_WIDTH), F32),
            pltpu.VMEM((n_seq, CONF_WIDTH), F32),
        ],
        input_output_aliases=aliases,
        compiler_params=pltpu.CompilerParams(dimension_semantics=("arbitrary",), vmem_limit_bytes=VMEM_LIMIT),
        name=f"even_sample_{e}",
    )(*args)


def _odd_prompt_kernel(n_t, x_ref, w_in_ref, w_out_ref, cw_ref, lnw_ref, lnb_ref, y_ref, z_out_ref, zbuf_ref):
    t_idx = pl.program_id(1)
    tb = x_ref.shape[0]
    w = SC_WIDTH

    @pl.when(t_idx == 0)
    def _():
        _zero_history(zbuf_ref, SC_HIST)

    x = x_ref[...]
    p = _dot(x.astype(BF16), w_in_ref[...])
    bg = p[:, 0:w]
    conv, hist = _causal_dwconv(zbuf_ref, p[:, w:2 * w] * p[:, 2 * w:3 * w], cw_ref, SC_HIST)

    @pl.when(t_idx == n_t - 1)
    def _():
        z_out_ref[...] = hist[SC_HIST - (SC_KERNEL - 1):, :]

    y = _dot((bg * conv).astype(BF16), w_out_ref[...])
    y_ref[...] = _layernorm(ALPHA * x + y, lnw_ref[...], lnb_ref[...])


def _odd_prompt(x, o, layer, w_in, w_out, cw, lnw, lnb):
    bsz, t_len, d = x.shape
    tb = min(TB_ODD, t_len)
    assert t_len % tb == 0 and t_len >= SC_KERNEL - 1
    n_t = t_len // tb
    return pl.pallas_call(
        functools.partial(_odd_prompt_kernel, n_t),
        grid=(bsz, n_t),
        in_specs=[
            pl.BlockSpec((None, tb, d), lambda b, t: (b, t, 0)),
            pl.BlockSpec((None, d, 3 * SC_WIDTH), lambda b, t: (o, 0, 0)),
            pl.BlockSpec((None, SC_WIDTH, d), lambda b, t: (o, 0, 0)),
            pl.BlockSpec((None, SC_KERNEL, SC_WIDTH), lambda b, t: (o, 0, 0)),
            pl.BlockSpec((None, 1, d), lambda b, t: (layer, 0, 0)),
            pl.BlockSpec((None, 1, d), lambda b, t: (layer, 0, 0)),
        ],
        out_specs=[
            pl.BlockSpec((None, tb, d), lambda b, t: (b, t, 0)),
            pl.BlockSpec((None, SC_KERNEL - 1, SC_WIDTH), lambda b, t: (b, 0, 0)),
        ],
        out_shape=[
            jax.ShapeDtypeStruct((bsz, t_len, d), F32),
            jax.ShapeDtypeStruct((bsz, SC_KERNEL - 1, SC_WIDTH), F32),
        ],
        scratch_shapes=[pltpu.VMEM((SC_WIDTH // LANES, ROW_PITCH * (SC_HIST + tb), LANES), F32)],
        compiler_params=pltpu.CompilerParams(dimension_semantics=("arbitrary", "arbitrary"),
                                             vmem_limit_bytes=VMEM_LIMIT),
        name=f"odd_prompt_{o}",
    )(x, w_in, w_out, cw, lnw, lnb)


def _odd_sample_kernel(x_ref, w_in_ref, w_out_ref, cw_ref, lnw_ref, lnb_ref, s_ref, y_ref, s_out_ref):
    w = SC_WIDTH
    x = x_ref[...]
    p = _dot(x.astype(BF16), w_in_ref[...])
    z = p[:, w:2 * w] * p[:, 2 * w:3 * w]
    s0 = s_ref[:, 0:w]
    s1 = s_ref[:, w:2 * w]
    conv = cw_ref[0:1, :] * s0 + cw_ref[1:2, :] * s1 + cw_ref[2:3, :] * z
    s_out_ref[:, 0:w] = s1
    s_out_ref[:, w:2 * w] = z
    y = _dot((p[:, 0:w] * conv).astype(BF16), w_out_ref[...])
    y_ref[...] = _layernorm(ALPHA * x + y, lnw_ref[...], lnb_ref[...])


def _odd_sample(x, o, layer, w_in, w_out, cw, lnw, lnb, state):
    n_seq, d = x.shape
    assert SC_KERNEL == 3
    hist = (SC_KERNEL - 1) * SC_WIDTH
    return pl.pallas_call(
        _odd_sample_kernel,
        grid=(1,),
        in_specs=[
            pl.BlockSpec((n_seq, d), lambda i: (0, 0)),
            pl.BlockSpec((None, d, 3 * SC_WIDTH), lambda i: (o, 0, 0)),
            pl.BlockSpec((None, SC_WIDTH, d), lambda i: (o, 0, 0)),
            pl.BlockSpec((None, SC_KERNEL, SC_WIDTH), lambda i: (o, 0, 0)),
            pl.BlockSpec((None, 1, d), lambda i: (layer, 0, 0)),
            pl.BlockSpec((None, 1, d), lambda i: (layer, 0, 0)),
            pl.BlockSpec((None, n_seq, hist), lambda i: (o, 0, 0)),
        ],
        out_specs=[
            pl.BlockSpec((n_seq, d), lambda i: (0, 0)),
            pl.BlockSpec((n_seq, hist), lambda i: (0, 0)),
        ],
        out_shape=[
            jax.ShapeDtypeStruct((n_seq, d), F32),
            jax.ShapeDtypeStruct((n_seq, hist), F32),
        ],
        compiler_params=pltpu.CompilerParams(dimension_semantics=("arbitrary",), vmem_limit_bytes=VMEM_LIMIT),
        name=f"odd_sample_{o}",
    )(x, w_in, w_out, cw, lnw, lnb, state)


def _ffn_kernel(n_p, xp_ref, xs_ref, w1_ref, w3_ref, w2_ref, lnw_ref, lnb_ref, yp_ref, ys_ref, pre_ref):
    i = pl.program_id(0)

    def residual_swiglu(x):
        xb = x.astype(BF16)
        h1 = _dot(xb, w1_ref[...])
        h3 = _dot(xb, w3_ref[...])
        return ALPHA * x + _dot((_silu(h1) * h3).astype(BF16), w2_ref[...])

    @pl.when(i == 0)
    def _():
        pre_ref[...] = jnp.zeros_like(pre_ref)

    @pl.when(i < n_p)
    def _():
        yp_ref[...] = _layernorm(pre_ref[...], lnw_ref[...], lnb_ref[...])
        pre_ref[...] = residual_swiglu(xp_ref[...])

    @pl.when(i == n_p)
    def _():
        yp_ref[...] = _layernorm(pre_ref[...], lnw_ref[...], lnb_ref[...])
        ys_ref[...] = _layernorm(residual_swiglu(xs_ref[...]), lnw_ref[...], lnb_ref[...])


def _ffn(xp, xs, layer, w1, w3, w2, lnw, lnb):
    m, d = xp.shape
    tm = min(TM_FFN, m)
    assert m % tm == 0
    n_p = m // tm
    prompt_block = pl.BlockSpec((tm, d), lambda i: (jnp.minimum(i, n_p - 1), 0))
    sample_block = pl.BlockSpec(xs.shape, lambda i: (0, 0))
    return pl.pallas_call(
        functools.partial(_ffn_kernel, n_p),
        grid=(n_p + 1,),
        in_specs=[
            prompt_block,
            sample_block,
            pl.BlockSpec((None, d, D_FF), lambda i: (layer, 0, 0)),
            pl.BlockSpec((None, d, D_FF), lambda i: (layer, 0, 0)),
            pl.BlockSpec((None, D_FF, d), lambda i: (layer, 0, 0)),
            pl.BlockSpec((None, 1, d), lambda i: (layer, 0, 0)),
            pl.BlockSpec((None, 1, d), lambda i: (layer, 0, 0)),
        ],
        out_specs=[pl.BlockSpec((tm, d), lambda i: (jnp.maximum(i - 1, 0), 0)), sample_block],
        out_shape=[jax.ShapeDtypeStruct((m, d), F32), jax.ShapeDtypeStruct(xs.shape, F32)],
        scratch_shapes=[pltpu.VMEM((tm, d), F32)],
        compiler_params=pltpu.CompilerParams(dimension_semantics=("arbitrary",), vmem_limit_bytes=VMEM_LIMIT),
        name=f"ffn_{layer}",
    )(xp, xs, w1, w3, w2, lnw, lnb)


def kernel(x_prompt, x_sample, state_hgrn, state_conf, state_sconv, w_in_even, w_out_even, hgrn_lb_logits,
           hgrn_gnorm_w, conf_dw_w, conf_dw_b, conf_ln_w, conf_ln_b, sc_w_in, sc_conv_w, sc_w_out,
           ffn_w1, ffn_w3, ffn_w2, ln_mix_w, ln_mix_b, ln_ffn_w, ln_ffn_b):
    bsz, t_len, d = x_prompt.shape
    n_seq = x_sample.shape[0]
    n_even, n_odd = state_hgrn.shape[0], state_sconv.shape[0]
    assert d == D_MODEL and x_sample.shape[1] == 1

    w_in_even = w_in_even.astype(BF16)
    w_out_even = w_out_even.astype(BF16)
    sc_w_in = sc_w_in.astype(BF16)
    sc_w_out = sc_w_out.astype(BF16)
    ffn_w1 = ffn_w1.astype(BF16)
    ffn_w3 = ffn_w3.astype(BF16)
    ffn_w2 = ffn_w2.astype(BF16)
    row = lambda a: a.reshape(a.shape[0], 1, a.shape[1])
    gn, dwb, clw, clb = row(hgrn_gnorm_w), row(conf_dw_b), row(conf_ln_w), row(conf_ln_b)
    lmw, lmb, lfw, lfb = row(ln_mix_w), row(ln_mix_b), row(ln_ffn_w), row(ln_ffn_b)
    sconv_flat = state_sconv.reshape(n_odd, n_seq, (SC_KERNEL - 1) * SC_WIDTH)
    conf_tmajor = jnp.transpose(state_conf, (0, 2, 1, 3))
    dall, lvl = _chunk_constants()

    xp = x_prompt
    xs = x_sample.reshape(n_seq, d)
    h_p, c_p, s_p, s_s = [], [], [], []
    h_s = c_s = None
    for layer in range(DEPTH):
        if layer % 2 == 0:
            e = layer // 2
            xp, sh, cb = _even_prompt(xp, e, layer, w_in_even, w_out_even, hgrn_lb_logits, gn, conf_dw_w, dwb,
                                      clw, clb, lmw, lmb, dall, lvl)
            h_p.append(sh)
            c_p.append(cb)
            xs, h_s, c_s = _even_sample(xs, e, layer, w_in_even, w_out_even, hgrn_lb_logits, gn, conf_dw_w, dwb,
                                        clw, clb, lmw, lmb, state_hgrn, conf_tmajor, h_s, c_s)
        else:
            o = layer // 2
            xp, sb = _odd_prompt(xp, o, layer, sc_w_in, sc_w_out, sc_conv_w, lmw, lmb)
            s_p.append(sb)
            xs, sb = _odd_sample(xs, o, layer, sc_w_in, sc_w_out, sc_conv_w, lmw, lmb, sconv_flat)
            s_s.append(sb.reshape(n_seq, SC_KERNEL - 1, SC_WIDTH))
        xp, xs = _ffn(xp.reshape(bsz * t_len, d), xs, layer, ffn_w1, ffn_w3, ffn_w2, lfw, lfb)
        xp = xp.reshape(bsz, t_len, d)
    return (xp, xs.reshape(n_seq, 1, d), jnp.stack(h_p), jnp.stack(c_p), jnp.stack(s_p), h_s,
            jnp.transpose(c_s, (0, 2, 1, 3)), jnp.stack(s_s))
```

```python
import functools
import math

import numpy as np
import jax
import jax.numpy as jnp
from jax import lax
from jax.experimental import pallas as pl
from jax.experimental.pallas import tpu as pltpu

F32 = jnp.float32
BF16 = jnp.bfloat16

D_MODEL = 1024
DEPTH = 4
HGRN_WIDTH = D_MODEL // 2
HGRN_DK = 128
HGRN_DV = 128
HGRN_HEADS = HGRN_WIDTH // HGRN_DK
CONF_WIDTH = D_MODEL - HGRN_WIDTH
CONF_KERNEL = 31
SC_WIDTH = D_MODEL
SC_KERNEL = 3
D_FF = 256 * math.ceil(8 * D_MODEL / 3 / 256)
EVEN_IN = 4 * HGRN_WIDTH + 2 * CONF_WIDTH
ALPHA = (2 * DEPTH) ** 0.25
LOG2_E = math.log2(math.e)
LN_EPS = 1e-5
RMS_EPS = 1e-6

SUBLANES = 8
LANES = 128
CHUNK = 128
N_LEVELS = CHUNK.bit_length() - 1
N_MXU_LEVELS = SUBLANES.bit_length() - 1
TB_EVEN = 512
TB_ODD = 512
TM_FFN = 512
CONF_HIST = 32
SC_HIST = SUBLANES
ROW_PITCH = 2
SAMPLE_GROUP = 8
VMEM_LIMIT = 56 * 1024 * 1024


def _sigmoid(x):
    return 1.0 / (1.0 + jnp.exp(-x))


def _silu(x):
    return x * _sigmoid(x)


def _layernorm(x, w, b):
    mu = jnp.mean(x, axis=-1, keepdims=True)
    xc = x - mu
    var = jnp.mean(xc * xc, axis=-1, keepdims=True)
    return xc * lax.rsqrt(var + LN_EPS) * w + b


def _dot(a, b):
    return jnp.dot(a, b, preferred_element_type=F32)


def _dot_nt(a, b):
    return lax.dot_general(a, b, (((1,), (1,)), ((), ())), preferred_element_type=F32)


def _dot_tn(a, b):
    return lax.dot_general(a, b, (((0,), (0,)), ((), ())), preferred_element_type=F32)


def _split3(x):
    hi = x.astype(BF16)
    r1 = x - hi.astype(F32)
    mid = r1.astype(BF16)
    lo = (r1 - mid.astype(F32)).astype(BF16)
    return hi, mid, lo


def _lower_bound(logits, e):
    m = jnp.max(logits, axis=0, keepdims=True)
    ex = jnp.exp(logits - m)
    sm = ex / jnp.sum(ex, axis=0, keepdims=True)
    lb = jnp.zeros_like(m)
    for i in range(1, e + 1):
        lb = lb + sm[i:i + 1]
    return lb


def _hgrn_gates(zq, zf, lb):
    t = jnp.exp(-jnp.abs(zf))
    s = 1.0 + t
    r = 1.0 / s
    sig_neg = jnp.where(zf >= 0.0, t * r, r)
    log_sig = jnp.minimum(zf, 0.0) - jnp.log(s)
    a = jnp.log(lb)
    c = jnp.log1p(-lb) + log_sig
    logf = jnp.maximum(a, c) + jnp.log(1.0 + jnp.exp(-jnp.abs(a - c)))
    kk = (1.0 - lb) * sig_neg
    qq = _silu(zq) * (HGRN_DK ** -0.5)
    return qq, kk, logf


def _level_operand(q, k, e_j, j):
    L, width = q.shape
    half = 1 << (j - 1)
    if half >= SUBLANES:
        pieces = []
        for s in range(0, L, 2 * half):
            pieces.append(k[s:s + half])
            pieces.append(q[s + half:s + 2 * half])
        base = jnp.concatenate(pieces, axis=0)
    else:
        sub = lax.broadcasted_iota(jnp.int32, (1, SUBLANES, width), 1)
        upper = ((sub >> (j - 1)) & 1) == 1
        shape3 = (L // SUBLANES, SUBLANES, width)
        base = jnp.where(upper, q.reshape(shape3), k.reshape(shape3)).reshape(L, width)
    return (base * e_j).astype(BF16)


def _block_diag(a):
    z = jnp.zeros((a.shape[0], LANES), a.dtype)
    return jnp.concatenate([jnp.concatenate([a[:, :LANES], z], axis=1),
                            jnp.concatenate([z, a[:, LANES:]], axis=1)], axis=0)


def _chunk_constants():
    L = CHUNK
    t = np.arange(L)[:, None]
    r = np.arange(L)[None, :]
    mats = []
    for j in range(1, N_MXU_LEVELS + 1):
        half = 1 << (j - 1)
        start = (t >> j) << j
        mid = start + half - 1
        upper = t >= start + half
        d = np.where(upper, (r > mid) & (r <= t), (r > t) & (r <= mid))
        mats.append(d)
    mats.append(r <= t)
    d_all = np.concatenate(mats, axis=0).astype(np.float32)
    x = np.bitwise_xor(t, r)
    lvl = np.zeros((L, L), np.int32)
    nz = x > 0
    lvl[nz] = np.floor(np.log2(x[nz])).astype(np.int32) + 1
    lvl = np.where(r > t, -1, lvl).astype(np.int32)
    d_all = np.concatenate([d_all] * 3, axis=1)
    lvl = np.concatenate([lvl, lvl], axis=1)
    return jnp.asarray(d_all, dtype=BF16), jnp.asarray(lvl)


def _pitched(start, n):
    return pl.ds(ROW_PITCH * start, n, stride=ROW_PITCH)


def _zero_history(buf_ref, n_hist):
    for c in range(buf_ref.shape[0]):
        buf_ref[c, _pitched(0, n_hist), :] = jnp.zeros((n_hist, LANES), F32)


def _causal_dwconv(buf_ref, new_rows, w_ref, n_hist):
    tiles = [_causal_dwconv_tile(buf_ref, c, new_rows, w_ref, n_hist) for c in range(new_rows.shape[1] // LANES)]
    return jnp.concatenate([t[0] for t in tiles], axis=-1), jnp.concatenate([t[1] for t in tiles], axis=-1)


def _causal_dwconv_tile(buf_ref, c, new_rows, w_ref, n_hist):
    tb = new_rows.shape[0]
    taps = w_ref.shape[0]
    off = n_hist - (taps - 1)
    cols = slice(c * LANES, (c + 1) * LANES)
    buf_ref[c, _pitched(n_hist, tb), :] = new_rows[:, cols]
    acc = w_ref[taps - 1:taps, cols] * new_rows[:, cols]
    for j in range(taps - 1):
        acc = acc + w_ref[j:j + 1, cols] * buf_ref[c, _pitched(off + j, tb), :]
    hist = buf_ref[c, _pitched(tb, n_hist), :]
    buf_ref[c, _pitched(0, n_hist), :] = hist
    return acc, hist


def _even_prompt_kernel(e, n_t, x_ref, w_in_ref, w_out_ref, logit_ref, gn_ref, dww_ref, dwb_ref, clw_ref, clb_ref,
                        lnw_ref, lnb_ref, dall_ref, lvl_ref, y_ref, s_out_ref, conf_out_ref, st_ref, ubuf_ref):
    t_idx = pl.program_id(1)
    tb = x_ref.shape[0]
    hw = HGRN_WIDTH

    @pl.when(t_idx == 0)
    def _():
        st_ref[...] = jnp.zeros_like(st_ref)
        _zero_history(ubuf_ref, CONF_HIST)

    x = x_ref[...]
    xb = x.astype(BF16)
    L = CHUNK
    n_tiles = CONF_WIDTH // LANES

    pc = _dot(xb, w_in_ref[:, 4 * hw:])
    u = pc[:, :CONF_WIDTH] * _sigmoid(pc[:, CONF_WIDTH:])
    conv_tiles = [_causal_dwconv_tile(ubuf_ref, 0, u, dww_ref, CONF_HIST)]
    pq = _dot(xb, w_in_ref[:, 0:2 * hw])
    conv_tiles.append(_causal_dwconv_tile(ubuf_ref, 1, u, dww_ref, CONF_HIST))
    pv = _dot(xb, w_in_ref[:, 2 * hw:4 * hw])
    vi = pv[:, 0:hw].astype(BF16)
    zg = pv[:, hw:]

    def conv_finish():
        conv = jnp.concatenate([t[0] for t in conv_tiles], axis=-1)
        cc = _silu(_layernorm(conv + dwb_ref[...], clw_ref[...], clb_ref[...]))
        return _dot(cc.astype(BF16), w_out_ref[hw:, :])

    fill = [lambda c=c: conv_tiles.append(_causal_dwconv_tile(ubuf_ref, c, u, dww_ref, CONF_HIST))
            for c in range(2, n_tiles)] + [conv_finish]
    fill_out = []

    lb = _lower_bound(logit_ref[...], e)
    lvl = lvl_ref[...]
    dall = dall_ref[...]
    gn = gn_ref[...]
    n_pairs = HGRN_HEADS // 2

    def exponents(c):
        rows = slice(c * L, (c + 1) * L)
        qq, kk, logf = _hgrn_gates(pq[rows, 0:hw], pq[rows, hw:], lb)
        g3 = jnp.concatenate(_split3(logf * LOG2_E), axis=0)
        return qq, kk, _dot(dall, g3)

    def chunk(c, qq, kk, eb):
        rows = slice(c * L, (c + 1) * L)
        b = eb[N_MXU_LEVELS * L:, :]
        e_lvl = [jnp.exp2(eb[(j - 1) * L:j * L, :]) for j in range(1, N_MXU_LEVELS + 1)]
        for j in range(N_MXU_LEVELS + 1, N_LEVELS + 1):
            size, half = 1 << j, 1 << (j - 1)
            pieces = []
            for s in range(0, L, size):
                b_m = b[s + half - 1:s + half, :]
                pieces += [b_m - b[s:s + half, :], b[s + half:s + size, :] - b_m]
            e_lvl.append(jnp.exp2(jnp.concatenate(pieces, axis=0)))
        e_q_all = jnp.exp2(b)
        e_k_all = jnp.exp2(b[L - 1:L, :] - b)
        cols = [slice(2 * hp * HGRN_DK, 2 * (hp + 1) * HGRN_DK) for hp in range(n_pairs)]
        q2 = [qq[:, cs] for cs in cols]
        k2 = [kk[:, cs] for cs in cols]
        scores = [jnp.where(lvl == 0, _dot_nt(q2[hp].astype(BF16), _block_diag(k2[hp].astype(BF16))), 0.0)
                  for hp in range(n_pairs)]
        for j in range(1, N_LEVELS + 1):
            for hp in range(n_pairs):
                m_j = _level_operand(q2[hp], k2[hp], e_lvl[j - 1][:, cols[hp]], j)
                scores[hp] = jnp.where(lvl == j, _dot_nt(m_j, _block_diag(m_j)), scores[hp])
        o_heads = []
        for hp in range(n_pairs):
            v2 = vi[rows, cols[hp]]
            e_q = e_q_all[:, cols[hp]]
            qd = (q2[hp] * e_q).astype(BF16)
            kd = (k2[hp] * e_k_all[:, cols[hp]]).astype(BF16)
            st = [st_ref[2 * hp], st_ref[2 * hp + 1]]
            st2 = jnp.concatenate([st[0].astype(BF16), st[1].astype(BF16)], axis=1)
            o2 = _dot(scores[hp].astype(BF16), _block_diag(v2)) + _dot_nt(qd, _block_diag(st2))
            for i in range(2):
                hc = slice(i * HGRN_DK, (i + 1) * HGRN_DK)
                st_ref[2 * hp + i] = e_q[L - 1:L, hc] * st[i] + _dot_tn(v2[:, hc], kd[:, hc])
                o_h = o2[:, hc]
                o_heads.append(o_h * lax.rsqrt(jnp.mean(o_h * o_h, axis=-1, keepdims=True) + RMS_EPS) * gn)
        return jnp.concatenate(o_heads, axis=-1) * _silu(zg[rows])

    n_chunks = tb // L
    o_chunks = []
    ahead = exponents(0)
    for c in range(n_chunks):
        cur = ahead
        if c + 1 < n_chunks:
            ahead = exponents(c + 1)
        o_chunks.append(chunk(c, *cur))
        if fill:
            fill_out.append(fill.pop(0)())
    while fill:
        fill_out.append(fill.pop(0)())
    y = fill_out[-1]
    o = jnp.concatenate(o_chunks, axis=0)

    @pl.when(t_idx == n_t - 1)
    def _():
        hist = jnp.concatenate([t[1] for t in conv_tiles], axis=-1)
        conf_out_ref[...] = hist[CONF_HIST - (CONF_KERNEL - 1):, :]
        for h in range(HGRN_HEADS):
            s_out_ref[h] = st_ref[h].T

    y = y + _dot(o.astype(BF16), w_out_ref[0:hw, :])
    y_ref[...] = _layernorm(ALPHA * x + y, lnw_ref[...], lnb_ref[...])


def _even_prompt(x, e, layer, w_in, w_out, logits, gn, dww, dwb, clw, clb, lnw, lnb, dall, lvl):
    bsz, t_len, d = x.shape
    tb = min(TB_EVEN, t_len)
    assert t_len % tb == 0 and tb % CHUNK == 0 and t_len >= CONF_KERNEL - 1
    n_t = t_len // tb
    const2 = lambda b, t: (0, 0)
    return pl.pallas_call(
        functools.partial(_even_prompt_kernel, e, n_t),
        grid=(bsz, n_t),
        in_specs=[
            pl.BlockSpec((None, tb, d), lambda b, t: (b, t, 0)),
            pl.BlockSpec((None, d, EVEN_IN), lambda b, t: (e, 0, 0)),
            pl.BlockSpec((None, d, d), lambda b, t: (e, 0, 0)),
            pl.BlockSpec(logits.shape, const2),
            pl.BlockSpec((None, 1, HGRN_DV), lambda b, t: (e, 0, 0)),
            pl.BlockSpec((None, CONF_KERNEL, CONF_WIDTH), lambda b, t: (e, 0, 0)),
            pl.BlockSpec((None, 1, CONF_WIDTH), lambda b, t: (e, 0, 0)),
            pl.BlockSpec((None, 1, CONF_WIDTH), lambda b, t: (e, 0, 0)),
            pl.BlockSpec((None, 1, CONF_WIDTH), lambda b, t: (e, 0, 0)),
            pl.BlockSpec((None, 1, d), lambda b, t: (layer, 0, 0)),
            pl.BlockSpec((None, 1, d), lambda b, t: (layer, 0, 0)),
            pl.BlockSpec(dall.shape, const2),
            pl.BlockSpec(lvl.shape, const2),
        ],
        out_specs=[
            pl.BlockSpec((None, tb, d), lambda b, t: (b, t, 0)),
            pl.BlockSpec((None, HGRN_HEADS, HGRN_DK, HGRN_DV), lambda b, t: (b, 0, 0, 0)),
            pl.BlockSpec((None, CONF_KERNEL - 1, CONF_WIDTH), lambda b, t: (b, 0, 0)),
        ],
        out_shape=[
            jax.ShapeDtypeStruct((bsz, t_len, d), F32),
            jax.ShapeDtypeStruct((bsz, HGRN_HEADS, HGRN_DK, HGRN_DV), F32),
            jax.ShapeDtypeStruct((bsz, CONF_KERNEL - 1, CONF_WIDTH), F32),
        ],
        scratch_shapes=[
            pltpu.VMEM((HGRN_HEADS, HGRN_DV, HGRN_DK), F32),
            pltpu.VMEM((CONF_WIDTH // LANES, ROW_PITCH * (CONF_HIST + tb), LANES), F32),
        ],
        compiler_params=pltpu.CompilerParams(dimension_semantics=("arbitrary", "arbitrary"),
                                             vmem_limit_bytes=VMEM_LIMIT),
        name=f"even_prompt_{e}",
    )(x, w_in, w_out, logits, gn, dww, dwb, clw, clb, lnw, lnb, dall, lvl)


def _even_sample_kernel(e, n_g, aliased, x_ref, w_in_ref, w_out_ref, logit_ref, gn_ref, dww_ref, dwb_ref, clw_ref,
                        clb_ref, lnw_ref, lnb_ref, s_ref, c_ref, *rest):
    if aliased:
        rest = rest[2:]
    y_ref, s_out_ref, c_out_ref, qt_ref, at_ref, kt_ref, v_ref, gate_ref, u_ref, o_ref, cacc_ref = rest
    i = pl.program_id(0)
    hw = HGRN_WIDTH
    n_seq = x_ref.shape[0]

    @pl.when(i == 0)
    def _():
        p = _dot(x_ref[...].astype(BF16), w_in_ref[...])
        lb = _lower_bound(logit_ref[...], e)
        qq, kk, logf = _hgrn_gates(p[:, 0:hw], p[:, hw:2 * hw], lb)
        qt_ref[...] = qq.T
        kt_ref[...] = kk.T
        at_ref[...] = jnp.exp(logf).T
        v_ref[...] = p[:, 2 * hw:3 * hw]
        gate_ref[...] = _silu(p[:, 3 * hw:4 * hw])
        u_ref[...] = p[:, 4 * hw:4 * hw + CONF_WIDTH] * _sigmoid(p[:, 4 * hw + CONF_WIDTH:])

    lane = lax.broadcasted_iota(jnp.int32, (HGRN_DK, n_seq), 1)
    n_hist = CONF_KERNEL - 1
    grp = pl.ds(pl.multiple_of(i * SAMPLE_GROUP, SAMPLE_GROUP), SAMPLE_GROUP)
    v_grp = v_ref[grp, :]
    u_grp = u_ref[grp, :]

    conv = dww_ref[n_hist:n_hist + 1, :] * u_grp + dwb_ref[...]
    for j in range(n_hist):
        conv = conv + dww_ref[j:j + 1, :] * c_ref[j]
    cacc_ref[grp, :] = conv
    c_out_ref[0:n_hist - 1] = c_ref[1:n_hist]
    c_out_ref[n_hist - 1] = u_grp

    o_rows = []
    for j in range(SAMPLE_GROUP):
        pick = lane == i * SAMPLE_GROUP + j
        o_heads = []
        for h in range(HGRN_HEADS):
            cols = slice(h * HGRN_DK, (h + 1) * HGRN_DK)
            a_col = jnp.sum(jnp.where(pick, at_ref[cols, :], 0.0), axis=1, keepdims=True)
            k_col = jnp.sum(jnp.where(pick, kt_ref[cols, :], 0.0), axis=1, keepdims=True)
            q_col = jnp.sum(jnp.where(pick, qt_ref[cols, :], 0.0), axis=1, keepdims=True)
            s_new = a_col * s_ref[j, h] + k_col * v_grp[j:j + 1, cols]
            s_out_ref[j, h] = s_new
            o_heads.append(jnp.sum(q_col * s_new, axis=0, keepdims=True))
        o_rows.append(jnp.concatenate(o_heads, axis=-1))
    o_ref[grp, :] = jnp.concatenate(o_rows, axis=0)

    @pl.when(i == n_g - 1)
    def _():
        gn = gn_ref[...]
        o_heads = []
        for h in range(HGRN_HEADS):
            o_h = o_ref[:, h * HGRN_DK:(h + 1) * HGRN_DK]
            o_heads.append(o_h * lax.rsqrt(jnp.mean(o_h * o_h, axis=-1, keepdims=True) + RMS_EPS) * gn)
        o = jnp.concatenate(o_heads, axis=-1) * gate_ref[...]
        cc = _silu(_layernorm(cacc_ref[...], clw_ref[...], clb_ref[...]))
        y = _dot(jnp.concatenate([o, cc], axis=-1).astype(BF16), w_out_ref[...])
        y_ref[...] = _layernorm(ALPHA * x_ref[...] + y, lnw_ref[...], lnb_ref[...])


def _even_sample(x, e, layer, w_in, w_out, logits, gn, dww, dwb, clw, clb, lnw, lnb, state_hgrn, state_conf,
                 prev_h=None, prev_c=None):
    n_seq, d = x.shape
    g = SAMPLE_GROUP
    assert n_seq % g == 0 and n_seq % 128 == 0
    n_g = n_seq // g
    aliased = prev_h is not None
    const2 = lambda i: (0, 0)
    in_specs = [
        pl.BlockSpec((n_seq, d), const2),
        pl.BlockSpec((None, d, EVEN_IN), lambda i: (e, 0, 0)),
        pl.BlockSpec((None, d, d), lambda i: (e, 0, 0)),
        pl.BlockSpec(logits.shape, const2),
        pl.BlockSpec((None, 1, HGRN_DV), lambda i: (e, 0, 0)),
        pl.BlockSpec((None, CONF_KERNEL, CONF_WIDTH), lambda i: (e, 0, 0)),
        pl.BlockSpec((None, 1, CONF_WIDTH), lambda i: (e, 0, 0)),
        pl.BlockSpec((None, 1, CONF_WIDTH), lambda i: (e, 0, 0)),
        pl.BlockSpec((None, 1, CONF_WIDTH), lambda i: (e, 0, 0)),
        pl.BlockSpec((None, 1, d), lambda i: (layer, 0, 0)),
        pl.BlockSpec((None, 1, d), lambda i: (layer, 0, 0)),
        pl.BlockSpec((None, g, HGRN_HEADS, HGRN_DK, HGRN_DV), lambda i: (e, i, 0, 0, 0)),
        pl.BlockSpec((None, CONF_KERNEL - 1, g, CONF_WIDTH), lambda i: (e, 0, i, 0)),
    ]
    args =[x, w_in, w_out, logits, gn, dww, dwb, clw, clb, lnw, lnb, state_hgrn, state_conf]
    aliases = {}
    if aliased:
        in_specs += [pl.BlockSpec(memory_space=pl.ANY), pl.BlockSpec(memory_space=pl.ANY)]
        aliases = {len(args): 1, len(args) + 1: 2}
        args += [prev_h, prev_c]
    return pl.pallas_call(
        functools.partial(_even_sample_kernel, e, n_g, aliased),
        grid=(n_g,),
        in_specs=in_specs,
        out_specs=[
            pl.BlockSpec((n_seq, d), const2),
            pl.BlockSpec((None, g, HGRN_HEADS, HGRN_DK, HGRN_DV), lambda i: (e, i, 0, 0, 0)),
            pl.BlockSpec((None, CONF_KERNEL - 1, g, CONF_WIDTH), lambda i: (e, 0, i, 0)),
        ],
        out_shape=[
            jax.ShapeDtypeStruct((n_seq, d), F32),
            jax.ShapeDtypeStruct(state_hgrn.shape, F32),
            jax.ShapeDtypeStruct(state_conf.shape, F32),
        ],
        scratch_shapes=[
            pltpu.VMEM((HGRN_WIDTH, n_seq), F32),
            pltpu.VMEM((HGRN_WIDTH, n_seq), F32),
            pltpu.VMEM((HGRN_WIDTH, n_seq), F32),
            pltpu.VMEM((n_seq, HGRN_WIDTH), F32),
            pltpu.VMEM((n_seq, HGRN_WIDTH), F32),
            pltpu.VMEM((n_seq, CONF_WIDTH), F32),
            pltpu.VMEM((n_seq, HGRN_WIDTH), F32),
            pltpu.VMEM((n_seq, CONF_WIDTH), F32),
        ],
        input_output_aliases=aliases,
        compiler_params=pltpu.CompilerParams(dimension_semantics=("arbitrary",), vmem_limit_bytes=VMEM_LIMIT),
        name=f"even_sample_{e}",
    )(*args)


def _odd_prompt_kernel(n_t, x_ref, w_in_ref, w_out_ref, cw_ref, lnw_ref, lnb_ref, y_ref, z_out_ref, zbuf_ref):
    t_idx = pl.program_id(1)
    tb = x_ref.shape[0]
    w = SC_WIDTH

    @pl.when(t_idx == 0)
    def _():
        _zero_history(zbuf_ref, SC_HIST)

    x = x_ref[...]
    xb = x.astype(BF16)
    z = _dot(xb, w_in_ref[:, w:2 * w]) * _dot(xb, w_in_ref[:, 2 * w:3 * w])
    conv, hist = _causal_dwconv(zbuf_ref, z, cw_ref, SC_HIST)
    bg = _dot(xb, w_in_ref[:, 0:w])

    @pl.when(t_idx == n_t - 1)
    def _():
        z_out_ref[...] = hist[SC_HIST - (SC_KERNEL - 1):, :]

    y = _dot((bg * conv).astype(BF16), w_out_ref[...])
    y_ref[...] = _layernorm(ALPHA * x + y, lnw_ref[...], lnb_ref[...])


def _odd_prompt(x, o, layer, w_in, w_out, cw, lnw, lnb):
    bsz, t_len, d = x.shape
    tb = min(TB_ODD, t_len)
    assert t_len % tb == 0 and t_len >= SC_KERNEL - 1
    n_t = t_len // tb
    return pl.pallas_call(
        functools.partial(_odd_prompt_kernel, n_t),
        grid=(bsz, n_t),
        in_specs=[
            pl.BlockSpec((None, tb, d), lambda b, t: (b, t, 0)),
            pl.BlockSpec((None, d, 3 * SC_WIDTH), lambda b, t: (o, 0, 0)),
            pl.BlockSpec((None, SC_WIDTH, d), lambda b, t: (o, 0, 0)),
            pl.BlockSpec((None, SC_KERNEL, SC_WIDTH), lambda b, t: (o, 0, 0)),
            pl.BlockSpec((None, 1, d), lambda b, t: (layer, 0, 0)),
            pl.BlockSpec((None, 1, d), lambda b, t: (layer, 0, 0)),
        ],
        out_specs=[
            pl.BlockSpec((None, tb, d), lambda b, t: (b, t, 0)),
            pl.BlockSpec((None, SC_KERNEL - 1, SC_WIDTH), lambda b, t: (b, 0, 0)),
        ],
        out_shape=[
            jax.ShapeDtypeStruct((bsz, t_len, d), F32),
            jax.ShapeDtypeStruct((bsz, SC_KERNEL - 1, SC_WIDTH), F32),
        ],
        scratch_shapes=[pltpu.VMEM((SC_WIDTH // LANES, ROW_PITCH * (SC_HIST + tb), LANES), F32)],
        compiler_params=pltpu.CompilerParams(dimension_semantics=("arbitrary", "arbitrary"),
                                             vmem_limit_bytes=VMEM_LIMIT),
        name=f"odd_prompt_{o}",
    )(x, w_in, w_out, cw, lnw, lnb)


def _odd_sample_kernel(x_ref, w_in_ref, w_out_ref, cw_ref, lnw_ref, lnb_ref, s_ref, y_ref, s_out_ref):
    w = SC_WIDTH
    x = x_ref[...]
    p = _dot(x.astype(BF16), w_in_ref[...])
    z = p[:, w:2 * w] * p[:, 2 * w:3 * w]
    s0 = s_ref[:, 0:w]
    s1 = s_ref[:, w:2 * w]
    conv = cw_ref[0:1, :] * s0 + cw_ref[1:2, :] * s1 + cw_ref[2:3, :] * z
    s_out_ref[:, 0:w] = s1
    s_out_ref[:, w:2 * w] = z
    y = _dot((p[:, 0:w] * conv).astype(BF16), w_out_ref[...])
    y_ref[...] = _layernorm(ALPHA * x + y, lnw_ref[...], lnb_ref[...])


def _odd_sample(x, o, layer, w_in, w_out, cw, lnw, lnb, state):
    n_seq, d = x.shape
    assert SC_KERNEL == 3
    hist = (SC_KERNEL - 1) * SC_WIDTH
    return pl.pallas_call(
        _odd_sample_kernel,
        grid=(1,),
        in_specs=[
            pl.BlockSpec((n_seq, d), lambda i: (0, 0)),
            pl.BlockSpec((None, d, 3 * SC_WIDTH), lambda i: (o, 0, 0)),
            pl.BlockSpec((None, SC_WIDTH, d), lambda i: (o, 0, 0)),
            pl.BlockSpec((None, SC_KERNEL, SC_WIDTH), lambda i: (o, 0, 0)),
            pl.BlockSpec((None, 1, d), lambda i: (layer, 0, 0)),
            pl.BlockSpec((None, 1, d), lambda i: (layer, 0, 0)),
            pl.BlockSpec((None, n_seq, hist), lambda i: (o, 0, 0)),
        ],
        out_specs=[
            pl.BlockSpec((n_seq, d), lambda i: (0, 0)),
            pl.BlockSpec((n_seq, hist), lambda i: (0, 0)),
        ],
        out_shape=[
            jax.ShapeDtypeStruct((n_seq, d), F32),
            jax.ShapeDtypeStruct((n_seq, hist), F32),
        ],
        compiler_params=pltpu.CompilerParams(dimension_semantics=("arbitrary",), vmem_limit_bytes=VMEM_LIMIT),
        name=f"odd_sample_{o}",
    )(x, w_in, w_out, cw, lnw, lnb, state)


def _ffn_kernel(n_p, xp_ref, xs_ref, w1_ref, w3_ref, w2_ref, lnw_ref, lnb_ref, yp_ref, ys_ref, pre_ref):
    i = pl.program_id(0)

    def residual_swiglu(x):
        xb = x.astype(BF16)
        h1 = _dot(xb, w1_ref[...])
        h3 = _dot(xb, w3_ref[...])
        return ALPHA * x + _dot((_silu(h1) * h3).astype(BF16), w2_ref[...])

    @pl.when(i == 0)
    def _():
        pre_ref[...] = jnp.zeros_like(pre_ref)

    @pl.when(i < n_p)
    def _():
        yp_ref[...] = _layernorm(pre_ref[...], lnw_ref[...], lnb_ref[...])
        pre_ref[...] = residual_swiglu(xp_ref[...])

    @pl.when(i == n_p)
    def _():
        yp_ref[...] = _layernorm(pre_ref[...], lnw_ref[...], lnb_ref[...])
        ys_ref[...] = _layernorm(residual_swiglu(xs_ref[...]), lnw_ref[...], lnb_ref[...])


def _ffn(xp, xs, layer, w1, w3, w2, lnw, lnb):
    m, d = xp.shape
    tm = min(TM_FFN, m)
    assert m % tm == 0
    n_p = m // tm
    prompt_block = pl.BlockSpec((tm, d), lambda i: (jnp.minimum(i, n_p - 1), 0))
    sample_block = pl.BlockSpec(xs.shape, lambda i: (0, 0))
    return pl.pallas_call(
        functools.partial(_ffn_kernel, n_p),
        grid=(n_p + 1,),
        in_specs=[
            prompt_block,
            sample_block,
            pl.BlockSpec((None, d, D_FF), lambda i: (layer, 0, 0)),
            pl.BlockSpec((None, d, D_FF), lambda i: (layer, 0, 0)),
            pl.BlockSpec((None, D_FF, d), lambda i: (layer, 0, 0)),
            pl.BlockSpec((None, 1, d), lambda i: (layer, 0, 0)),
            pl.BlockSpec((None, 1, d), lambda i: (layer, 0, 0)),
        ],
        out_specs=[pl.BlockSpec((tm, d), lambda i: (jnp.maximum(i - 1, 0), 0)), sample_block],
        out_shape=[jax.ShapeDtypeStruct((m, d), F32), jax.ShapeDtypeStruct(xs.shape, F32)],
        scratch_shapes=[pltpu.VMEM((tm, d), F32)],
        compiler_params=pltpu.CompilerParams(dimension_semantics=("arbitrary",), vmem_limit_bytes=VMEM_LIMIT),
        name=f"ffn_{layer}",
    )(xp, xs, w1, w3, w2, lnw, lnb)


def kernel(x_prompt, x_sample, state_hgrn, state_conf, state_sconv, w_in_even, w_out_even, hgrn_lb_logits,
           hgrn_gnorm_w, conf_dw_w, conf_dw_b, conf_ln_w, conf_ln_b, sc_w_in, sc_conv_w, sc_w_out,
           ffn_w1, ffn_w3, ffn_w2, ln_mix_w, ln_mix_b, ln_ffn_w, ln_ffn_b):
    bsz, t_len, d = x_prompt.shape
    n_seq = x_sample.shape[0]
    n_even, n_odd = state_hgrn.shape[0], state_sconv.shape[0]
    assert d == D_MODEL and x_sample.shape[1] == 1

    w_in_even = w_in_even.astype(BF16)
    w_out_even = w_out_even.astype(BF16)
    sc_w_in = sc_w_in.astype(BF16)
    sc_w_out = sc_w_out.astype(BF16)
    ffn_w1 = ffn_w1.astype(BF16)
    ffn_w3 = ffn_w3.astype(BF16)
    ffn_w2 = ffn_w2.astype(BF16)
    row = lambda a: a.reshape(a.shape[0], 1, a.shape[1])
    gn, dwb, clw, clb = row(hgrn_gnorm_w), row(conf_dw_b), row(conf_ln_w), row(conf_ln_b)
    lmw, lmb, lfw, lfb = row(ln_mix_w), row(ln_mix_b), row(ln_ffn_w), row(ln_ffn_b)
    sconv_flat = state_sconv.reshape(n_odd, n_seq, (SC_KERNEL - 1) * SC_WIDTH)
    conf_tmajor = jnp.transpose(state_conf, (0, 2, 1, 3))
    dall, lvl = _chunk_constants()

    xp = x_prompt
    xs = x_sample.reshape(n_seq, d)
    h_p, c_p, s_p, s_s = [], [], [], []
    h_s = c_s = None
    for layer in range(DEPTH):
        if layer % 2 == 0:
            e = layer // 2
            xp, sh, cb = _even_prompt(xp, e, layer, w_in_even, w_out_even, hgrn_lb_logits, gn, conf_dw_w, dwb,
                                      clw, clb, lmw, lmb, dall, lvl)
            h_p.append(sh)
            c_p.append(cb)
            xs, h_s, c_s = _even_sample(xs, e, layer, w_in_even, w_out_even, hgrn_lb_logits, gn, conf_dw_w, dwb,
                                        clw, clb, lmw, lmb, state_hgrn, conf_tmajor, h_s, c_s)
        else:
            o = layer // 2
            xp, sb = _odd_prompt(xp, o, layer, sc_w_in, sc_w_out, sc_conv_w, lmw, lmb)
            s_p.append(sb)
            xs, sb = _odd_sample(xs, o, layer, sc_w_in, sc_w_out, sc_conv_w, lmw, lmb, sconv_flat)
            s_s.append(sb.reshape(n_seq, SC_KERNEL - 1, SC_WIDTH))
        xp, xs = _ffn(xp.reshape(bsz * t_len, d), xs, layer, ffn_w1, ffn_w3, ffn_w2, lfw, lfb)
        xp = xp.reshape(bsz, t_len, d)
    return (xp, xs.reshape(n_seq, 1, d), jnp.stack(h_p), jnp.stack(c_p), jnp.stack(s_p), h_s,
            jnp.transpose(c_s, (0, 2, 1, 3)), jnp.stack(s_s))
```

```python
import functools
import math

import numpy as np
import jax
import jax.numpy as jnp
from jax import lax
from jax.experimental import pallas as pl
from jax.experimental.pallas import tpu as pltpu

F32 = jnp.float32
BF16 = jnp.bfloat16

D_MODEL = 1024
DEPTH = 4
HGRN_WIDTH = D_MODEL // 2
HGRN_DK = 128
HGRN_DV = 128
HGRN_HEADS = HGRN_WIDTH // HGRN_DK
CONF_WIDTH = D_MODEL - HGRN_WIDTH
CONF_KERNEL = 31
SC_WIDTH = D_MODEL
SC_KERNEL = 3
D_FF = 256 * math.ceil(8 * D_MODEL / 3 / 256)
EVEN_IN = 4 * HGRN_WIDTH + 2 * CONF_WIDTH
ALPHA = (2 * DEPTH) ** 0.25
LOG2_E = math.log2(math.e)
LN_EPS = 1e-5
RMS_EPS = 1e-6

SUBLANES = 8
LANES = 128
CHUNK = 128
N_LEVELS = CHUNK.bit_length() - 1
N_MXU_LEVELS = SUBLANES.bit_length() - 1
TB_EVEN = 512
TB_ODD = 512
TM_FFN = 512
CONF_HIST = 32
SC_HIST = SUBLANES
ROW_PITCH = 2
SAMPLE_GROUP = 8
VMEM_LIMIT = 56 * 1024 * 1024


def _sigmoid(x):
    return 1.0 / (1.0 + jnp.exp(-x))


def _silu(x):
    return x * _sigmoid(x)


def _layernorm(x, w, b):
    mu = jnp.mean(x, axis=-1, keepdims=True)
    xc = x - mu
    var = jnp.mean(xc * xc, axis=-1, keepdims=True)
    return xc * lax.rsqrt(var + LN_EPS) * w + b


def _dot(a, b):
    return jnp.dot(a, b, preferred_element_type=F32)


def _dot_nt(a, b):
    return lax.dot_general(a, b, (((1,), (1,)), ((), ())), preferred_element_type=F32)


def _dot_tn(a, b):
    return lax.dot_general(a, b, (((0,), (0,)), ((), ())), preferred_element_type=F32)


def _split3(x):
    hi = x.astype(BF16)
    r1 = x - hi.astype(F32)
    mid = r1.astype(BF16)
    lo = (r1 - mid.astype(F32)).astype(BF16)
    return hi, mid, lo


def _lower_bound(logits, e):
    m = jnp.max(logits, axis=0, keepdims=True)
    ex = jnp.exp(logits - m)
    sm = ex / jnp.sum(ex, axis=0, keepdims=True)
    lb = jnp.zeros_like(m)
    for i in range(1, e + 1):
        lb = lb + sm[i:i + 1]
    return lb


def _hgrn_gates(zq, zf, lb):
    t = jnp.exp(-jnp.abs(zf))
    s = 1.0 + t
    r = 1.0 / s
    sig_neg = jnp.where(zf >= 0.0, t * r, r)
    log_sig = jnp.minimum(zf, 0.0) - jnp.log(s)
    a = jnp.log(lb)
    c = jnp.log1p(-lb) + log_sig
    logf = jnp.maximum(a, c) + jnp.log(1.0 + jnp.exp(-jnp.abs(a - c)))
    kk = (1.0 - lb) * sig_neg
    qq = _silu(zq) * (HGRN_DK ** -0.5)
    return qq, kk, logf


def _level_operand(q, k, e_j, j):
    L, width = q.shape
    half = 1 << (j - 1)
    if half >= SUBLANES:
        pieces = []
        for s in range(0, L, 2 * half):
            pieces.append(k[s:s + half])
            pieces.append(q[s + half:s + 2 * half])
        base = jnp.concatenate(pieces, axis=0)
    else:
        sub = lax.broadcasted_iota(jnp.int32, (1, SUBLANES, width), 1)
        upper = ((sub >> (j - 1)) & 1) == 1
        shape3 = (L // SUBLANES, SUBLANES, width)
        base = jnp.where(upper, q.reshape(shape3), k.reshape(shape3)).reshape(L, width)
    return (base * e_j).astype(BF16)


def _block_diag(a):
    z = jnp.zeros((a.shape[0], LANES), a.dtype)
    return jnp.concatenate([jnp.concatenate([a[:, :LANES], z], axis=1),
                            jnp.concatenate([z, a[:, LANES:]], axis=1)], axis=0)


def _chunk_constants():
    L = CHUNK
    t = np.arange(L)[:, None]
    r = np.arange(L)[None, :]
    mats = []
    for j in range(1, N_MXU_LEVELS + 1):
        half = 1 << (j - 1)
        start = (t >> j) << j
        mid = start + half - 1
        upper = t >= start + half
        d = np.where(upper, (r > mid) & (r <= t), (r > t) & (r <= mid))
        mats.append(d)
    mats.append(r <= t)
    d_all = np.concatenate(mats, axis=0).astype(np.float32)
    x = np.bitwise_xor(t, r)
    lvl = np.zeros((L, L), np.int32)
    nz = x > 0
    lvl[nz] = np.floor(np.log2(x[nz])).astype(np.int32) + 1
    lvl = np.where(r > t, -1, lvl).astype(np.int32)
    d_all = np.concatenate([d_all] * 3, axis=1)
    lvl = np.concatenate([lvl, lvl], axis=1)
    return jnp.asarray(d_all, dtype=BF16), jnp.asarray(lvl)


def _pitched(start, n):
    return pl.ds(ROW_PITCH * start, n, stride=ROW_PITCH)


def _zero_history(buf_ref, n_hist):
    for c in range(buf_ref.shape[0]):
        buf_ref[c, _pitched(0, n_hist), :] = jnp.zeros((n_hist, LANES), F32)


def _causal_dwconv(buf_ref, new_rows, w_ref, n_hist):
    tiles = [_causal_dwconv_tile(buf_ref, c, new_rows, w_ref, n_hist) for c in range(new_rows.shape[1] // LANES)]
    return jnp.concatenate([t[0] for t in tiles], axis=-1), jnp.concatenate([t[1] for t in tiles], axis=-1)


def _causal_dwconv_tile(buf_ref, c, new_rows, w_ref, n_hist):
    tb = new_rows.shape[0]
    taps = w_ref.shape[0]
    off = n_hist - (taps - 1)
    cols = slice(c * LANES, (c + 1) * LANES)
    buf_ref[c, _pitched(n_hist, tb), :] = new_rows[:, cols]
    acc = w_ref[taps - 1:taps, cols] * new_rows[:, cols]
    for j in range(taps - 1):
        acc = acc + w_ref[j:j + 1, cols] * buf_ref[c, _pitched(off + j, tb), :]
    hist = buf_ref[c, _pitched(tb, n_hist), :]
    buf_ref[c, _pitched(0, n_hist), :] = hist
    return acc, hist


def _even_prompt_kernel(e, n_t, x_ref, w_in_ref, w_out_ref, logit_ref, gn_ref, dww_ref, dwb_ref, clw_ref, clb_ref,
                        lnw_ref, lnb_ref, dall_ref, lvl_ref, y_ref, s_out_ref, conf_out_ref, st_ref, ubuf_ref):
    t_idx = pl.program_id(1)
    tb = x_ref.shape[0]
    hw = HGRN_WIDTH

    @pl.when(t_idx == 0)
    def _():
        st_ref[...] = jnp.zeros_like(st_ref)
        _zero_history(ubuf_ref, CONF_HIST)

    x = x_ref[...]
    xb = x.astype(BF16)
    L = CHUNK
    n_tiles = CONF_WIDTH // LANES

    pc = _dot(xb, w_in_ref[:, 4 * hw:])
    u = pc[:, :CONF_WIDTH] * _sigmoid(pc[:, CONF_WIDTH:])
    conv_tiles = [_causal_dwconv_tile(ubuf_ref, 0, u, dww_ref, CONF_HIST)]
    pq = _dot(xb, w_in_ref[:, 0:2 * hw])
    conv_tiles.append(_causal_dwconv_tile(ubuf_ref, 1, u, dww_ref, CONF_HIST))
    pv = _dot(xb, w_in_ref[:, 2 * hw:4 * hw])
    vi = pv[:, 0:hw].astype(BF16)
    zg = pv[:, hw:]

    def conv_finish():
        conv = jnp.concatenate([t[0] for t in conv_tiles], axis=-1)
        cc = _silu(_layernorm(conv + dwb_ref[...], clw_ref[...], clb_ref[...]))
        return _dot(cc.astype(BF16), w_out_ref[hw:, :])

    fill = [lambda c=c: conv_tiles.append(_causal_dwconv_tile(ubuf_ref, c, u, dww_ref, CONF_HIST))
            for c in range(2, n_tiles)] + [conv_finish]
    fill_out = []

    lb = _lower_bound(logit_ref[...], e)
    lvl = lvl_ref[...]
    dall = dall_ref[...]
    gn = gn_ref[...]
    n_pairs = HGRN_HEADS // 2

    def exponents(c):
        rows = slice(c * L, (c + 1) * L)
        qq, kk, logf = _hgrn_gates(pq[rows, 0:hw], pq[rows, hw:], lb)
        g3 = jnp.concatenate(_split3(logf * LOG2_E), axis=0)
        return qq, kk, _dot(dall, g3)

    def chunk(c, qq, kk, eb):
        rows = slice(c * L, (c + 1) * L)
        b = eb[N_MXU_LEVELS * L:, :]
        e_lvl = [jnp.exp2(eb[(j - 1) * L:j * L, :]) for j in range(1, N_MXU_LEVELS + 1)]
        for j in range(N_MXU_LEVELS + 1, N_LEVELS + 1):
            size, half = 1 << j, 1 << (j - 1)
            pieces = []
            for s in range(0, L, size):
                b_m = b[s + half - 1:s + half, :]
                pieces += [b_m - b[s:s + half, :], b[s + half:s + size, :] - b_m]
            e_lvl.append(jnp.exp2(jnp.concatenate(pieces, axis=0)))
        e_q_all = jnp.exp2(b)
        e_k_all = jnp.exp2(b[L - 1:L, :] - b)
        cols = [slice(2 * hp * HGRN_DK, 2 * (hp + 1) * HGRN_DK) for hp in range(n_pairs)]
        q2 = [qq[:, cs] for cs in cols]
        k2 = [kk[:, cs] for cs in cols]
        scores = [jnp.where(lvl == 0, _dot_nt(q2[hp].astype(BF16), _block_diag(k2[hp].astype(BF16))), 0.0)
                  for hp in range(n_pairs)]
        for j in range(1, N_LEVELS + 1):
            for hp in range(n_pairs):
                m_j = _level_operand(q2[hp], k2[hp], e_lvl[j - 1][:, cols[hp]], j)
                scores[hp] = jnp.where(lvl == j, _dot_nt(m_j, _block_diag(m_j)), scores[hp])
        o_heads = []
        for hp in range(n_pairs):
            v2 = vi[rows, cols[hp]]
            e_q = e_q_all[:, cols[hp]]
            qd = (q2[hp] * e_q).astype(BF16)
            kd = (k2[hp] * e_k_all[:, cols[hp]]).astype(BF16)
            st = [st_ref[2 * hp], st_ref[2 * hp + 1]]
            st2 = jnp.concatenate([st[0].astype(BF16), st[1].astype(BF16)], axis=1)
            o2 = _dot(scores[hp].astype(BF16), _block_diag(v2)) + _dot_nt(qd, _block_diag(st2))
            for i in range(2):
                hc = slice(i * HGRN_DK, (i + 1) * HGRN_DK)
                st_ref[2 * hp + i] = e_q[L - 1:L, hc] * st[i] + _dot_tn(v2[:, hc], kd[:, hc])
                o_h = o2[:, hc]
                o_heads.append(o_h * lax.rsqrt(jnp.mean(o_h * o_h, axis=-1, keepdims=True) + RMS_EPS) * gn)
        return jnp.concatenate(o_heads, axis=-1) * _silu(zg[rows])

    n_chunks = tb // L
    o_chunks = []
    finished = 0

    def finish(upto, y_conv):
        rows = slice(finished * L, upto * L)
        o_rows = jnp.concatenate(o_chunks[finished:upto], axis=0)
        y = y_conv[rows] + _dot(o_rows.astype(BF16), w_out_ref[0:hw, :])
        y_ref[rows, :] = _layernorm(ALPHA * x[rows] + y, lnw_ref[...], lnb_ref[...])

    ahead = exponents(0)
    for c in range(n_chunks):
        cur = ahead
        if c + 1 < n_chunks:
            ahead = exponents(c + 1)
        o_chunks.append(chunk(c, *cur))
        if fill:
            fill_out.append(fill.pop(0)())
        if len(fill) == 1 and c + 1 < n_chunks:
            fill_out.append(fill.pop(0)())
        if not fill and c + 1 == n_chunks // 2 + 1 and n_chunks >= 4:
            finish(n_chunks // 2, fill_out[-1])
            finished = n_chunks // 2
    while fill:
        fill_out.append(fill.pop(0)())
    finish(n_chunks, fill_out[-1])

    @pl.when(t_idx == n_t - 1)
    def _():
        hist = jnp.concatenate([t[1] for t in conv_tiles], axis=-1)
        conf_out_ref[...] = hist[CONF_HIST - (CONF_KERNEL - 1):, :]
        for h in range(HGRN_HEADS):
            s_out_ref[h] = st_ref[h].T


def _even_prompt(x, e, layer, w_in, w_out, logits, gn, dww, dwb, clw, clb, lnw, lnb, dall, lvl):
    bsz, t_len, d = x.shape
    tb = min(TB_EVEN, t_len)
    assert t_len % tb == 0 and tb % CHUNK == 0 and t_len >= CONF_KERNEL - 1
    n_t = t_len // tb
    const2 = lambda b, t: (0, 0)
    return pl.pallas_call(
        functools.partial(_even_prompt_kernel, e, n_t),
        grid=(bsz, n_t),
        in_specs=[
            pl.BlockSpec((None, tb, d), lambda b, t: (b, t, 0)),
            pl.BlockSpec((None, d, EVEN_IN), lambda b, t: (e, 0, 0)),
            pl.BlockSpec((None, d, d), lambda b, t: (e, 0, 0)),
            pl.BlockSpec(logits.shape, const2),
            pl.BlockSpec((None, 1, HGRN_DV), lambda b, t: (e, 0, 0)),
            pl.BlockSpec((None, CONF_KERNEL, CONF_WIDTH), lambda b, t: (e, 0, 0)),
            pl.BlockSpec((None, 1, CONF_WIDTH), lambda b, t: (e, 0, 0)),
            pl.BlockSpec((None, 1, CONF_WIDTH), lambda b, t: (e, 0, 0)),
            pl.BlockSpec((None, 1, CONF_WIDTH), lambda b, t: (e, 0, 0)),
            pl.BlockSpec((None, 1, d), lambda b, t: (layer, 0, 0)),
            pl.BlockSpec((None, 1, d), lambda b, t: (layer, 0, 0)),
            pl.BlockSpec(dall.shape, const2),
            pl.BlockSpec(lvl.shape, const2),
        ],
        out_specs=[
            pl.BlockSpec((None, tb, d), lambda b, t: (b, t, 0)),
            pl.BlockSpec((None, HGRN_HEADS, HGRN_DK, HGRN_DV), lambda b, t: (b, 0, 0, 0)),
            pl.BlockSpec((None, CONF_KERNEL - 1, CONF_WIDTH), lambda b, t: (b, 0, 0)),
        ],
        out_shape=[
            jax.ShapeDtypeStruct((bsz, t_len, d), F32),
            jax.ShapeDtypeStruct((bsz, HGRN_HEADS, HGRN_DK, HGRN_DV), F32),
            jax.ShapeDtypeStruct((bsz, CONF_KERNEL - 1, CONF_WIDTH), F32),
        ],
        scratch_shapes=[
            pltpu.VMEM((HGRN_HEADS, HGRN_DV, HGRN_DK), F32),
            pltpu.VMEM((CONF_WIDTH // LANES, ROW_PITCH * (CONF_HIST + tb), LANES), F32),
        ],
        compiler_params=pltpu.CompilerParams(dimension_semantics=("arbitrary", "arbitrary"),
                                             vmem_limit_bytes=VMEM_LIMIT),
        name=f"even_prompt_{e}",
    )(x, w_in, w_out, logits, gn, dww, dwb, clw, clb, lnw, lnb, dall, lvl)


def _even_sample_kernel(e, n_g, aliased, x_ref, w_in_ref, w_out_ref, logit_ref, gn_ref, dww_ref, dwb_ref, clw_ref,
                        clb_ref, lnw_ref, lnb_ref, s_ref, c_ref, *rest):
    if aliased:
        rest = rest[2:]
    y_ref, s_out_ref, c_out_ref, qt_ref, at_ref, kt_ref, v_ref, gate_ref, u_ref, o_ref, cacc_ref = rest
    i = pl.program_id(0)
    hw = HGRN_WIDTH
    n_seq = x_ref.shape[0]

    @pl.when(i == 0)
    def _():
        p = _dot(x_ref[...].astype(BF16), w_in_ref[...])
        lb = _lower_bound(logit_ref[...], e)
        qq, kk, logf = _hgrn_gates(p[:, 0:hw], p[:, hw:2 * hw], lb)
        qt_ref[...] = qq.T
        kt_ref[...] = kk.T
        at_ref[...] = jnp.exp(logf).T
        v_ref[...] = p[:, 2 * hw:3 * hw]
        gate_ref[...] = _silu(p[:, 3 * hw:4 * hw])
        u_ref[...] = p[:, 4 * hw:4 * hw + CONF_WIDTH] * _sigmoid(p[:, 4 * hw + CONF_WIDTH:])

    lane = lax.broadcasted_iota(jnp.int32, (HGRN_DK, n_seq), 1)
    n_hist = CONF_KERNEL - 1
    grp = pl.ds(pl.multiple_of(i * SAMPLE_GROUP, SAMPLE_GROUP), SAMPLE_GROUP)
    v_grp = v_ref[grp, :]
    u_grp = u_ref[grp, :]

    conv = dww_ref[n_hist:n_hist + 1, :] * u_grp + dwb_ref[...]
    for j in range(n_hist):
        conv = conv + dww_ref[j:j + 1, :] * c_ref[j]
    cacc_ref[grp, :] = conv
    c_out_ref[0:n_hist - 1] = c_ref[1:n_hist]
    c_out_ref[n_hist - 1] = u_grp

    o_rows = []
    for j in range(SAMPLE_GROUP):
        pick = lane == i * SAMPLE_GROUP + j
        o_heads = []
        for h in range(HGRN_HEADS):
            cols = slice(h * HGRN_DK, (h + 1) * HGRN_DK)
            a_col = jnp.sum(jnp.where(pick, at_ref[cols, :], 0.0), axis=1, keepdims=True)
            k_col = jnp.sum(jnp.where(pick, kt_ref[cols, :], 0.0), axis=1, keepdims=True)
            q_col = jnp.sum(jnp.where(pick, qt_ref[cols, :], 0.0), axis=1, keepdims=True)
            s_new = a_col * s_ref[j, h] + k_col * v_grp[j:j + 1, cols]
            s_out_ref[j, h] = s_new
            o_heads.append(jnp.sum(q_col * s_new, axis=0, keepdims=True))
        o_rows.append(jnp.concatenate(o_heads, axis=-1))
    o_ref[grp, :] = jnp.concatenate(o_rows, axis=0)

    @pl.when(i == n_g - 1)
    def _():
        gn = gn_ref[...]
        o_heads = []
        for h in range(HGRN_HEADS):
            o_h = o_ref[:, h * HGRN_DK:(h + 1) * HGRN_DK]
            o_heads.append(o_h * lax.rsqrt(jnp.mean(o_h * o_h, axis=-1, keepdims=True) + RMS_EPS) * gn)
        o = jnp.concatenate(o_heads, axis=-1) * gate_ref[...]
        cc = _silu(_layernorm(cacc_ref[...], clw_ref[...], clb_ref[...]))
        y = _dot(jnp.concatenate([o, cc], axis=-1).astype(BF16), w_out_ref[...])
        y_ref[...] = _layernorm(ALPHA * x_ref[...] + y, lnw_ref[...], lnb_ref[...])


def _even_sample(x, e, layer, w_in, w_out, logits, gn, dww, dwb, clw, clb, lnw, lnb, state_hgrn, state_conf,
                 prev_h=None, prev_c=None):
    n_seq, d = x.shape
    g = SAMPLE_GROUP
    assert n_seq % g == 0 and n_seq % 128 == 0
    n_g = n_seq // g
    aliased = prev_h is not None
    const2 = lambda i: (0, 0)
    in_specs = [
        pl.BlockSpec((n_seq, d), const2),
        pl.BlockSpec((None, d, EVEN_IN), lambda i: (e, 0, 0)),
        pl.BlockSpec((None, d, d), lambda i: (e, 0, 0)),
        pl.BlockSpec(logits.shape, const2),
        pl.BlockSpec((None, 1, HGRN_DV), lambda i: (e, 0, 0)),
        pl.BlockSpec((None, CONF_KERNEL, CONF_WIDTH), lambda i: (e, 0, 0)),
        pl.BlockSpec((None, 1, CONF_WIDTH), lambda i: (e, 0, 0)),
        pl.BlockSpec((None, 1, CONF_WIDTH), lambda i: (e, 0, 0)),
        pl.BlockSpec((None, 1, CONF_WIDTH), lambda i: (e, 0, 0)),
        pl.BlockSpec((None, 1, d), lambda i: (layer, 0, 0)),
        pl.BlockSpec((None, 1, d), lambda i: (layer, 0, 0)),
        pl.BlockSpec((None, g, HGRN_HEADS, HGRN_DK, HGRN_DV), lambda i: (e, i, 0, 0, 0)),
        pl.BlockSpec((None, CONF_KERNEL - 1, g, CONF_WIDTH), lambda i: (e, 0, i, 0)),
    ]
    args =[x, w_in, w_out, logits, gn, dww, dwb, clw, clb, lnw, lnb, state_hgrn, state_conf]
    aliases = {}
    if aliased:
        in_specs += [pl.BlockSpec(memory_space=pl.ANY), pl.BlockSpec(memory_space=pl.ANY)]
        aliases = {len(args): 1, len(args) + 1: 2}
        args += [prev_h, prev_c]
    return pl.pallas_call(
        functools.partial(_even_sample_kernel, e, n_g, aliased),
        grid=(n_g,),
        in_specs=in_specs,
        out_specs=[
            pl.BlockSpec((n_seq, d), const2),
            pl.BlockSpec((None, g, HGRN_HEADS, HGRN_DK, HGRN_DV), lambda i: (e, i, 0, 0, 0)),
            pl.BlockSpec((None, CONF_KERNEL - 1, g, CONF_WIDTH), lambda i: (e, 0, i, 0)),
        ],
        out_shape=[
            jax.ShapeDtypeStruct((n_seq, d), F32),
            jax.ShapeDtypeStruct(state_hgrn.shape, F32),
            jax.ShapeDtypeStruct(state_conf.shape, F32),
        ],
        scratch_shapes=[
            pltpu.VMEM((HGRN_WIDTH, n_seq), F32),
            pltpu.VMEM((HGRN_WIDTH, n_seq), F32),
            pltpu.VMEM((HGRN_WIDTH, n_seq), F32),
            pltpu.VMEM((n_seq, HGRN_WIDTH), F32),
            pltpu.VMEM((n_seq, HGRN_WIDTH), F32),
            pltpu.VMEM((n_seq, CONF_WIDTH), F32),
            pltpu.VMEM((n_seq, HGRN_WIDTH), F32),
            pltpu.VMEM((n_seq, CONF_WIDTH), F32),
        ],
        input_output_aliases=aliases,
        compiler_params=pltpu.CompilerParams(dimension_semantics=("arbitrary",), vmem_limit_bytes=VMEM_LIMIT),
        name=f"even_sample_{e}",
    )(*args)


def _odd_prompt_kernel(n_t, x_ref, w_in_ref, w_out_ref, cw_ref, lnw_ref, lnb_ref, y_ref, z_out_ref, zbuf_ref):
    t_idx = pl.program_id(1)
    tb = x_ref.shape[0]
    w = SC_WIDTH

    @pl.when(t_idx == 0)
    def _():
        _zero_history(zbuf_ref, SC_HIST)

    x = x_ref[...]
    xb = x.astype(BF16)
    z = _dot(xb, w_in_ref[:, w:2 * w]) * _dot(xb, w_in_ref[:, 2 * w:3 * w])
    conv, hist = _causal_dwconv(zbuf_ref, z, cw_ref, SC_HIST)
    gated = (_dot(xb, w_in_ref[:, 0:w]) * conv).astype(BF16)
    n_groups = 2 if tb % (2 * SUBLANES) == 0 else 1
    m = tb // n_groups
    for r in range(n_groups):
        rows = slice(r * m, (r + 1) * m)
        y_ref[rows, :] = _layernorm(ALPHA * x[rows] + _dot(gated[rows], w_out_ref[...]), lnw_ref[...], lnb_ref[...])

    @pl.when(t_idx == n_t - 1)
    def _():
        z_out_ref[...] = hist[SC_HIST - (SC_KERNEL - 1):, :]


def _odd_prompt(x, o, layer, w_in, w_out, cw, lnw, lnb):
    bsz, t_len, d = x.shape
    tb = min(TB_ODD, t_len)
    assert t_len % tb == 0 and t_len >= SC_KERNEL - 1
    n_t = t_len // tb
    return pl.pallas_call(
        functools.partial(_odd_prompt_kernel, n_t),
        grid=(bsz, n_t),
        in_specs=[
            pl.BlockSpec((None, tb, d), lambda b, t: (b, t, 0)),
            pl.BlockSpec((None, d, 3 * SC_WIDTH), lambda b, t: (o, 0, 0)),
            pl.BlockSpec((None, SC_WIDTH, d), lambda b, t: (o, 0, 0)),
            pl.BlockSpec((None, SC_KERNEL, SC_WIDTH), lambda b, t: (o, 0, 0)),
            pl.BlockSpec((None, 1, d), lambda b, t: (layer, 0, 0)),
            pl.BlockSpec((None, 1, d), lambda b, t: (layer, 0, 0)),
        ],
        out_specs=[
            pl.BlockSpec((None, tb, d), lambda b, t: (b, t, 0)),
            pl.BlockSpec((None, SC_KERNEL - 1, SC_WIDTH), lambda b, t: (b, 0, 0)),
        ],
        out_shape=[
            jax.ShapeDtypeStruct((bsz, t_len, d), F32),
            jax.ShapeDtypeStruct((bsz, SC_KERNEL - 1, SC_WIDTH), F32),
        ],
        scratch_shapes=[pltpu.VMEM((SC_WIDTH // LANES, ROW_PITCH * (SC_HIST + tb), LANES), F32)],
        compiler_params=pltpu.CompilerParams(dimension_semantics=("arbitrary", "arbitrary"),
                                             vmem_limit_bytes=VMEM_LIMIT),
        name=f"odd_prompt_{o}",
    )(x, w_in, w_out, cw, lnw, lnb)


def _odd_sample_kernel(x_ref, w_in_ref, w_out_ref, cw_ref, lnw_ref, lnb_ref, s_ref, y_ref, s_out_ref):
    w = SC_WIDTH
    x = x_ref[...]
    p = _dot(x.astype(BF16), w_in_ref[...])
    z = p[:, w:2 * w] * p[:, 2 * w:3 * w]
    s0 = s_ref[:, 0:w]
    s1 = s_ref[:, w:2 * w]
    conv = cw_ref[0:1, :] * s0 + cw_ref[1:2, :] * s1 + cw_ref[2:3, :] * z
    s_out_ref[:, 0:w] = s1
    s_out_ref[:, w:2 * w] = z
    y = _dot((p[:, 0:w] * conv).astype(BF16), w_out_ref[...])
    y_ref[...] = _layernorm(ALPHA * x + y, lnw_ref[...], lnb_ref[...])


def _odd_sample(x, o, layer, w_in, w_out, cw, lnw, lnb, state):
    n_seq, d = x.shape
    assert SC_KERNEL == 3
    hist = (SC_KERNEL - 1) * SC_WIDTH
    return pl.pallas_call(
        _odd_sample_kernel,
        grid=(1,),
        in_specs=[
            pl.BlockSpec((n_seq, d), lambda i: (0, 0)),
            pl.BlockSpec((None, d, 3 * SC_WIDTH), lambda i: (o, 0, 0)),
            pl.BlockSpec((None, SC_WIDTH, d), lambda i: (o, 0, 0)),
            pl.BlockSpec((None, SC_KERNEL, SC_WIDTH), lambda i: (o, 0, 0)),
            pl.BlockSpec((None, 1, d), lambda i: (layer, 0, 0)),
            pl.BlockSpec((None, 1, d), lambda i: (layer, 0, 0)),
            pl.BlockSpec((None, n_seq, hist), lambda i: (o, 0, 0)),
        ],
        out_specs=[
            pl.BlockSpec((n_seq, d), lambda i: (0, 0)),
            pl.BlockSpec((n_seq, hist), lambda i: (0, 0)),
        ],
        out_shape=[
            jax.ShapeDtypeStruct((n_seq, d), F32),
            jax.ShapeDtypeStruct((n_seq, hist), F32),
        ],
        compiler_params=pltpu.CompilerParams(dimension_semantics=("arbitrary",), vmem_limit_bytes=VMEM_LIMIT),
        name=f"odd_sample_{o}",
    )(x, w_in, w_out, cw, lnw, lnb, state)


def _ffn_kernel(n_p, xp_ref, xs_ref, w1_ref, w3_ref, w2_ref, lnw_ref, lnb_ref, yp_ref, ys_ref, pre_ref):
    i = pl.program_id(0)

    def residual_swiglu(x):
        xb = x.astype(BF16)
        h1 = _dot(xb, w1_ref[...])
        h3 = _dot(xb, w3_ref[...])
        return ALPHA * x + _dot((_silu(h1) * h3).astype(BF16), w2_ref[...])

    @pl.when(i == 0)
    def _():
        pre_ref[...] = jnp.zeros_like(pre_ref)

    @pl.when(i < n_p)
    def _():
        yp_ref[...] = _layernorm(pre_ref[...], lnw_ref[...], lnb_ref[...])
        pre_ref[...] = residual_swiglu(xp_ref[...])

    @pl.when(i == n_p)
    def _():
        yp_ref[...] = _layernorm(pre_ref[...], lnw_ref[...], lnb_ref[...])
        ys_ref[...] = _layernorm(residual_swiglu(xs_ref[...]), lnw_ref[...], lnb_ref[...])


def _ffn(xp, xs, layer, w1, w3, w2, lnw, lnb):
    m, d = xp.shape
    tm = min(TM_FFN, m)
    assert m % tm == 0
    n_p = m // tm
    prompt_block = pl.BlockSpec((tm, d), lambda i: (jnp.minimum(i, n_p - 1), 0))
    sample_block = pl.BlockSpec(xs.shape, lambda i: (0, 0))
    return pl.pallas_call(
        functools.partial(_ffn_kernel, n_p),
        grid=(n_p + 1,),
        in_specs=[
            prompt_block,
            sample_block,
            pl.BlockSpec((None, d, D_FF), lambda i: (layer, 0, 0)),
            pl.BlockSpec((None, d, D_FF), lambda i: (layer, 0, 0)),
            pl.BlockSpec((None, D_FF, d), lambda i: (layer, 0, 0)),
            pl.BlockSpec((None, 1, d), lambda i: (layer, 0, 0)),
            pl.BlockSpec((None, 1, d), lambda i: (layer, 0, 0)),
        ],
        out_specs=[pl.BlockSpec((tm, d), lambda i: (jnp.maximum(i - 1, 0), 0)), sample_block],
        out_shape=[jax.ShapeDtypeStruct((m, d), F32), jax.ShapeDtypeStruct(xs.shape, F32)],
        scratch_shapes=[pltpu.VMEM((tm, d), F32)],
        compiler_params=pltpu.CompilerParams(dimension_semantics=("arbitrary",), vmem_limit_bytes=VMEM_LIMIT),
        name=f"ffn_{layer}",
    )(xp, xs, w1, w3, w2, lnw, lnb)


def kernel(x_prompt, x_sample, state_hgrn, state_conf, state_sconv, w_in_even, w_out_even, hgrn_lb_logits,
           hgrn_gnorm_w, conf_dw_w, conf_dw_b, conf_ln_w, conf_ln_b, sc_w_in, sc_conv_w, sc_w_out,
           ffn_w1, ffn_w3, ffn_w2, ln_mix_w, ln_mix_b, ln_ffn_w, ln_ffn_b):
    bsz, t_len, d = x_prompt.shape
    n_seq = x_sample.shape[0]
    n_even, n_odd = state_hgrn.shape[0], state_sconv.shape[0]
    assert d == D_MODEL and x_sample.shape[1] == 1

    w_in_even = w_in_even.astype(BF16)
    w_out_even = w_out_even.astype(BF16)
    sc_w_in = sc_w_in.astype(BF16)
    sc_w_out = sc_w_out.astype(BF16)
    ffn_w1 = ffn_w1.astype(BF16)
    ffn_w3 = ffn_w3.astype(BF16)
    ffn_w2 = ffn_w2.astype(BF16)
    row = lambda a: a.reshape(a.shape[0], 1, a.shape[1])
    gn, dwb, clw, clb = row(hgrn_gnorm_w), row(conf_dw_b), row(conf_ln_w), row(conf_ln_b)
    lmw, lmb, lfw, lfb = row(ln_mix_w), row(ln_mix_b), row(ln_ffn_w), row(ln_ffn_b)
    sconv_flat = state_sconv.reshape(n_odd, n_seq, (SC_KERNEL - 1) * SC_WIDTH)
    conf_tmajor = jnp.transpose(state_conf, (0, 2, 1, 3))
    dall, lvl = _chunk_constants()

    xp = x_prompt
    xs = x_sample.reshape(n_seq, d)
    h_p, c_p, s_p, s_s = [], [], [], []
    h_s = c_s = None
    for layer in range(DEPTH):
        if layer % 2 == 0:
            e = layer // 2
            xp, sh, cb = _even_prompt(xp, e, layer, w_in_even, w_out_even, hgrn_lb_logits, gn, conf_dw_w, dwb,
                                      clw, clb, lmw, lmb, dall, lvl)
            h_p.append(sh)
            c_p.append(cb)
            xs, h_s, c_s = _even_sample(xs, e, layer, w_in_even, w_out_even, hgrn_lb_logits, gn, conf_dw_w, dwb,
                                        clw, clb, lmw, lmb, state_hgrn, conf_tmajor, h_s, c_s)
        else:
            o = layer // 2
            xp, sb = _odd_prompt(xp, o, layer, sc_w_in, sc_w_out, sc_conv_w, lmw, lmb)
            s_p.append(sb)
            xs, sb = _odd_sample(xs, o, layer, sc_w_in, sc_w_out, sc_conv_w, lmw, lmb, sconv_flat)
            s_s.append(sb.reshape(n_seq, SC_KERNEL - 1, SC_WIDTH))
        xp, xs = _ffn(xp.reshape(bsz * t_len, d), xs, layer, ffn_w1, ffn_w3, ffn_w2, lfw, lfb)
        xp = xp.reshape(bsz, t_len, d)
    return (xp, xs.reshape(n_seq, 1, d), jnp.stack(h_p), jnp.stack(c_p), jnp.stack(s_p), h_s,
            jnp.transpose(c_s, (0, 2, 1, 3)), jnp.stack(s_s))
```

```python
import functools
import math

import numpy as np
import jax
import jax.numpy as jnp
from jax import lax
from jax.experimental import pallas as pl
from jax.experimental.pallas import tpu as pltpu

F32 = jnp.float32
BF16 = jnp.bfloat16

D_MODEL = 1024
DEPTH = 4
HGRN_WIDTH = D_MODEL // 2
HGRN_DK = 128
HGRN_DV = 128
HGRN_HEADS = HGRN_WIDTH // HGRN_DK
CONF_WIDTH = D_MODEL - HGRN_WIDTH
CONF_KERNEL = 31
SC_WIDTH = D_MODEL
SC_KERNEL = 3
D_FF = 256 * math.ceil(8 * D_MODEL / 3 / 256)
EVEN_IN = 4 * HGRN_WIDTH + 2 * CONF_WIDTH
ALPHA = (2 * DEPTH) ** 0.25
LOG2_E = math.log2(math.e)
LN_EPS = 1e-5
RMS_EPS = 1e-6

SUBLANES = 8
LANES = 128
CHUNK = 128
N_LEVELS = CHUNK.bit_length() - 1
N_MXU_LEVELS = SUBLANES.bit_length() - 1
TB_EVEN = 512
TB_ODD = 512
TM_FFN = 512
CONF_HIST = 32
SC_HIST = SUBLANES
ROW_PITCH = 3
SAMPLE_GROUP = 8
VMEM_LIMIT = 56 * 1024 * 1024


def _sigmoid(x):
    return 1.0 / (1.0 + jnp.exp(-x))


def _silu(x):
    return x * _sigmoid(x)


def _layernorm(x, w, b):
    mu = jnp.mean(x, axis=-1, keepdims=True)
    xc = x - mu
    var = jnp.mean(xc * xc, axis=-1, keepdims=True)
    return xc * lax.rsqrt(var + LN_EPS) * w + b


def _dot(a, b):
    return jnp.dot(a, b, preferred_element_type=F32)


def _dot_nt(a, b):
    return lax.dot_general(a, b, (((1,), (1,)), ((), ())), preferred_element_type=F32)


def _dot_tn(a, b):
    return lax.dot_general(a, b, (((0,), (0,)), ((), ())), preferred_element_type=F32)


def _split3(x):
    hi = x.astype(BF16)
    r1 = x - hi.astype(F32)
    mid = r1.astype(BF16)
    lo = (r1 - mid.astype(F32)).astype(BF16)
    return hi, mid, lo


def _lower_bound(logits, e):
    m = jnp.max(logits, axis=0, keepdims=True)
    ex = jnp.exp(logits - m)
    sm = ex / jnp.sum(ex, axis=0, keepdims=True)
    lb = jnp.zeros_like(m)
    for i in range(1, e + 1):
        lb = lb + sm[i:i + 1]
    return lb


def _hgrn_gates(zq, zf, lb):
    t = jnp.exp(-jnp.abs(zf))
    s = 1.0 + t
    r = 1.0 / s
    sig_neg = jnp.where(zf >= 0.0, t * r, r)
    log_sig = jnp.minimum(zf, 0.0) - jnp.log(s)
    a = jnp.log(lb)
    c = jnp.log1p(-lb) + log_sig
    logf = jnp.maximum(a, c) + jnp.log(1.0 + jnp.exp(-jnp.abs(a - c)))
    kk = (1.0 - lb) * sig_neg
    qq = _silu(zq) * (HGRN_DK ** -0.5)
    return qq, kk, logf


def _level_operand(q, k, e_j, j):
    L, width = q.shape
    half = 1 << (j - 1)
    if half >= SUBLANES:
        pieces = []
        for s in range(0, L, 2 * half):
            pieces.append(k[s:s + half])
            pieces.append(q[s + half:s + 2 * half])
        base = jnp.concatenate(pieces, axis=0)
    else:
        sub = lax.broadcasted_iota(jnp.int32, (1, SUBLANES, width), 1)
        upper = ((sub >> (j - 1)) & 1) == 1
        shape3 = (L // SUBLANES, SUBLANES, width)
        base = jnp.where(upper, q.reshape(shape3), k.reshape(shape3)).reshape(L, width)
    return (base * e_j).astype(BF16)


def _block_diag(a):
    z = jnp.zeros((a.shape[0], LANES), a.dtype)
    return jnp.concatenate([jnp.concatenate([a[:, :LANES], z], axis=1),
                            jnp.concatenate([z, a[:, LANES:]], axis=1)], axis=0)


def _chunk_constants():
    L = CHUNK
    t = np.arange(L)[:, None]
    r = np.arange(L)[None, :]
    mats = []
    for j in range(1, N_MXU_LEVELS + 1):
        half = 1 << (j - 1)
        start = (t >> j) << j
        mid = start + half - 1
        upper = t >= start + half
        d = np.where(upper, (r > mid) & (r <= t), (r > t) & (r <= mid))
        mats.append(d)
    mats.append(r <= t)
    d_all = np.concatenate(mats, axis=0).astype(np.float32)
    x = np.bitwise_xor(t, r)
    lvl = np.zeros((L, L), np.int32)
    nz = x > 0
    lvl[nz] = np.floor(np.log2(x[nz])).astype(np.int32) + 1
    lvl = np.where(r > t, -1, lvl).astype(np.int32)
    d_all = np.concatenate([d_all] * 3, axis=1)
    lvl = np.concatenate([lvl, lvl], axis=1)
    return jnp.asarray(d_all, dtype=BF16), jnp.asarray(lvl)


def _pitched(start, n):
    return pl.ds(ROW_PITCH * start, n, stride=ROW_PITCH)


def _zero_history(buf_ref, n_hist):
    for c in range(buf_ref.shape[0]):
        buf_ref[c, _pitched(0, n_hist), :] = jnp.zeros((n_hist, LANES), F32)


def _causal_dwconv(buf_ref, new_rows, w_ref, n_hist):
    tiles = [_causal_dwconv_tile(buf_ref, c, new_rows, w_ref, n_hist) for c in range(new_rows.shape[1] // LANES)]
    return jnp.concatenate([t[0] for t in tiles], axis=-1), jnp.concatenate([t[1] for t in tiles], axis=-1)


def _causal_dwconv_tile(buf_ref, c, new_rows, w_ref, n_hist):
    tb = new_rows.shape[0]
    taps = w_ref.shape[0]
    off = n_hist - (taps - 1)
    cols = slice(c * LANES, (c + 1) * LANES)
    buf_ref[c, _pitched(n_hist, tb), :] = new_rows[:, cols]
    acc = w_ref[taps - 1:taps, cols] * new_rows[:, cols]
    for j in range(taps - 1):
        acc = acc + w_ref[j:j + 1, cols] * buf_ref[c, _pitched(off + j, tb), :]
    hist = buf_ref[c, _pitched(tb, n_hist), :]
    buf_ref[c, _pitched(0, n_hist), :] = hist
    return acc, hist


def _even_prompt_kernel(e, n_t, x_ref, w_in_ref, w_out_ref, logit_ref, gn_ref, dww_ref, dwb_ref, clw_ref, clb_ref,
                        lnw_ref, lnb_ref, dall_ref, lvl_ref, y_ref, s_out_ref, conf_out_ref, st_ref, ubuf_ref):
    t_idx = pl.program_id(1)
    tb = x_ref.shape[0]
    hw = HGRN_WIDTH

    @pl.when(t_idx == 0)
    def _():
        st_ref[...] = jnp.zeros_like(st_ref)
        _zero_history(ubuf_ref, CONF_HIST)

    x = x_ref[...]
    xb = x.astype(BF16)
    L = CHUNK
    n_tiles = CONF_WIDTH // LANES

    pc = _dot(xb, w_in_ref[:, 4 * hw:])
    u = pc[:, :CONF_WIDTH] * _sigmoid(pc[:, CONF_WIDTH:])
    conv_tiles = [_causal_dwconv_tile(ubuf_ref, 0, u, dww_ref, CONF_HIST)]
    pq = _dot(xb, w_in_ref[:, 0:2 * hw])
    conv_tiles.append(_causal_dwconv_tile(ubuf_ref, 1, u, dww_ref, CONF_HIST))
    pv = _dot(xb, w_in_ref[:, 2 * hw:4 * hw])
    vi = pv[:, 0:hw].astype(BF16)
    zg = pv[:, hw:]

    def conv_finish():
        conv = jnp.concatenate([t[0] for t in conv_tiles], axis=-1)
        cc = _silu(_layernorm(conv + dwb_ref[...], clw_ref[...], clb_ref[...]))
        return _dot(cc.astype(BF16), w_out_ref[hw:, :])

    fill = [lambda c=c: conv_tiles.append(_causal_dwconv_tile(ubuf_ref, c, u, dww_ref, CONF_HIST))
            for c in range(2, n_tiles)] + [conv_finish]
    fill_out = []

    lb = _lower_bound(logit_ref[...], e)
    lvl = lvl_ref[...]
    dall = dall_ref[...]
    gn = gn_ref[...]
    n_pairs = HGRN_HEADS // 2

    def exponents(c):
        rows = slice(c * L, (c + 1) * L)
        qq, kk, logf = _hgrn_gates(pq[rows, 0:hw], pq[rows, hw:], lb)
        g3 = jnp.concatenate(_split3(logf * LOG2_E), axis=0)
        return qq, kk, _dot(dall, g3)

    def chunk(c, qq, kk, eb):
        rows = slice(c * L, (c + 1) * L)
        b = eb[N_MXU_LEVELS * L:, :]
        e_lvl = [jnp.exp2(eb[(j - 1) * L:j * L, :]) for j in range(1, N_MXU_LEVELS + 1)]
        for j in range(N_MXU_LEVELS + 1, N_LEVELS + 1):
            size, half = 1 << j, 1 << (j - 1)
            pieces = []
            for s in range(0, L, size):
                b_m = b[s + half - 1:s + half, :]
                pieces += [b_m - b[s:s + half, :], b[s + half:s + size, :] - b_m]
            e_lvl.append(jnp.exp2(jnp.concatenate(pieces, axis=0)))
        e_q_all = jnp.exp2(b)
        e_k_all = jnp.exp2(b[L - 1:L, :] - b)
        cols = [slice(2 * hp * HGRN_DK, 2 * (hp + 1) * HGRN_DK) for hp in range(n_pairs)]
        q2 = [qq[:, cs] for cs in cols]
        k2 = [kk[:, cs] for cs in cols]
        scores = [jnp.where(lvl == 0, _dot_nt(q2[hp].astype(BF16), _block_diag(k2[hp].astype(BF16))), 0.0)
                  for hp in range(n_pairs)]
        for j in range(1, N_LEVELS + 1):
            for hp in range(n_pairs):
                m_j = _level_operand(q2[hp], k2[hp], e_lvl[j - 1][:, cols[hp]], j)
                scores[hp] = jnp.where(lvl == j, _dot_nt(m_j, _block_diag(m_j)), scores[hp])
        o_heads = []
        for hp in range(n_pairs):
            v2 = vi[rows, cols[hp]]
            e_q = e_q_all[:, cols[hp]]
            qd = (q2[hp] * e_q).astype(BF16)
            kd = (k2[hp] * e_k_all[:, cols[hp]]).astype(BF16)
            st = [st_ref[2 * hp], st_ref[2 * hp + 1]]
            st2 = jnp.concatenate([st[0].astype(BF16), st[1].astype(BF16)], axis=1)
            o2 = _dot(scores[hp].astype(BF16), _block_diag(v2)) + _dot_nt(qd, _block_diag(st2))
            for i in range(2):
                hc = slice(i * HGRN_DK, (i + 1) * HGRN_DK)
                st_ref[2 * hp + i] = e_q[L - 1:L, hc] * st[i] + _dot_tn(v2[:, hc], kd[:, hc])
                o_h = o2[:, hc]
                o_heads.append(o_h * lax.rsqrt(jnp.mean(o_h * o_h, axis=-1, keepdims=True) + RMS_EPS) * gn)
        return jnp.concatenate(o_heads, axis=-1) * _silu(zg[rows])

    n_chunks = tb // L
    o_chunks = []
    finished = 0

    def finish(upto, y_conv):
        rows = slice(finished * L, upto * L)
        o_rows = jnp.concatenate(o_chunks[finished:upto], axis=0)
        y = y_conv[rows] + _dot(o_rows.astype(BF16), w_out_ref[0:hw, :])
        y_ref[rows, :] = _layernorm(ALPHA * x[rows] + y, lnw_ref[...], lnb_ref[...])

    ahead = exponents(0)
    for c in range(n_chunks):
        cur = ahead
        if c + 1 < n_chunks:
            ahead = exponents(c + 1)
        o_chunks.append(chunk(c, *cur))
        if fill:
            fill_out.append(fill.pop(0)())
        if len(fill) == 1 and c + 1 < n_chunks:
            fill_out.append(fill.pop(0)())
        if not fill and c + 1 == n_chunks // 2 + 1 and n_chunks >= 4:
            finish(n_chunks // 2, fill_out[-1])
            finished = n_chunks // 2
    while fill:
        fill_out.append(fill.pop(0)())
    finish(n_chunks, fill_out[-1])

    @pl.when(t_idx == n_t - 1)
    def _():
        hist = jnp.concatenate([t[1] for t in conv_tiles], axis=-1)
        conf_out_ref[...] = hist[CONF_HIST - (CONF_KERNEL - 1):, :]
        for h in range(HGRN_HEADS):
            s_out_ref[h] = st_ref[h].T


def _even_prompt(x, e, layer, w_in, w_out, logits, gn, dww, dwb, clw, clb, lnw, lnb, dall, lvl):
    bsz, t_len, d = x.shape
    tb = min(TB_EVEN, t_len)
    assert t_len % tb == 0 and tb % CHUNK == 0 and t_len >= CONF_KERNEL - 1
    n_t = t_len // tb
    const2 = lambda b, t: (0, 0)
    return pl.pallas_call(
        functools.partial(_even_prompt_kernel, e, n_t),
        grid=(bsz, n_t),
        in_specs=[
            pl.BlockSpec((None, tb, d), lambda b, t: (b, t, 0)),
            pl.BlockSpec((None, d, EVEN_IN), lambda b, t: (e, 0, 0)),
            pl.BlockSpec((None, d, d), lambda b, t: (e, 0, 0)),
            pl.BlockSpec(logits.shape, const2),
            pl.BlockSpec((None, 1, HGRN_DV), lambda b, t: (e, 0, 0)),
            pl.BlockSpec((None, CONF_KERNEL, CONF_WIDTH), lambda b, t: (e, 0, 0)),
            pl.BlockSpec((None, 1, CONF_WIDTH), lambda b, t: (e, 0, 0)),
            pl.BlockSpec((None, 1, CONF_WIDTH), lambda b, t: (e, 0, 0)),
            pl.BlockSpec((None, 1, CONF_WIDTH), lambda b, t: (e, 0, 0)),
            pl.BlockSpec((None, 1, d), lambda b, t: (layer, 0, 0)),
            pl.BlockSpec((None, 1, d), lambda b, t: (layer, 0, 0)),
            pl.BlockSpec(dall.shape, const2),
            pl.BlockSpec(lvl.shape, const2),
        ],
        out_specs=[
            pl.BlockSpec((None, tb, d), lambda b, t: (b, t, 0)),
            pl.BlockSpec((None, HGRN_HEADS, HGRN_DK, HGRN_DV), lambda b, t: (b, 0, 0, 0)),
            pl.BlockSpec((None, CONF_KERNEL - 1, CONF_WIDTH), lambda b, t: (b, 0, 0)),
        ],
        out_shape=[
            jax.ShapeDtypeStruct((bsz, t_len, d), F32),
            jax.ShapeDtypeStruct((bsz, HGRN_HEADS, HGRN_DK, HGRN_DV), F32),
            jax.ShapeDtypeStruct((bsz, CONF_KERNEL - 1, CONF_WIDTH), F32),
        ],
        scratch_shapes=[
            pltpu.VMEM((HGRN_HEADS, HGRN_DV, HGRN_DK), F32),
            pltpu.VMEM((CONF_WIDTH // LANES, ROW_PITCH * (CONF_HIST + tb), LANES), F32),
        ],
        compiler_params=pltpu.CompilerParams(dimension_semantics=("arbitrary", "arbitrary"),
                                             vmem_limit_bytes=VMEM_LIMIT),
        name=f"even_prompt_{e}",
    )(x, w_in, w_out, logits, gn, dww, dwb, clw, clb, lnw, lnb, dall, lvl)


def _even_sample_kernel(e, n_g, aliased, x_ref, w_in_ref, w_out_ref, logit_ref, gn_ref, dww_ref, dwb_ref, clw_ref,
                        clb_ref, lnw_ref, lnb_ref, s_ref, c_ref, *rest):
    if aliased:
        rest = rest[2:]
    y_ref, s_out_ref, c_out_ref, qt_ref, at_ref, kt_ref, v_ref, gate_ref, u_ref, o_ref, cacc_ref = rest
    i = pl.program_id(0)
    hw = HGRN_WIDTH
    n_seq = x_ref.shape[0]

    @pl.when(i == 0)
    def _():
        p = _dot(x_ref[...].astype(BF16), w_in_ref[...])
        lb = _lower_bound(logit_ref[...], e)
        qq, kk, logf = _hgrn_gates(p[:, 0:hw], p[:, hw:2 * hw], lb)
        qt_ref[...] = qq.T
        kt_ref[...] = kk.T
        at_ref[...] = jnp.exp(logf).T
        v_ref[...] = p[:, 2 * hw:3 * hw]
        gate_ref[...] = _silu(p[:, 3 * hw:4 * hw])
        u_ref[...] = p[:, 4 * hw:4 * hw + CONF_WIDTH] * _sigmoid(p[:, 4 * hw + CONF_WIDTH:])

    lane = lax.broadcasted_iota(jnp.int32, (HGRN_DK, n_seq), 1)
    n_hist = CONF_KERNEL - 1
    grp = pl.ds(pl.multiple_of(i * SAMPLE_GROUP, SAMPLE_GROUP), SAMPLE_GROUP)
    v_grp = v_ref[grp, :]
    u_grp = u_ref[grp, :]

    conv = dww_ref[n_hist:n_hist + 1, :] * u_grp + dwb_ref[...]
    for j in range(n_hist):
        conv = conv + dww_ref[j:j + 1, :] * c_ref[j]
    cacc_ref[grp, :] = conv
    c_out_ref[0:n_hist - 1] = c_ref[1:n_hist]
    c_out_ref[n_hist - 1] = u_grp

    o_rows = []
    for j in range(SAMPLE_GROUP):
        pick = lane == i * SAMPLE_GROUP + j
        o_heads = []
        for h in range(HGRN_HEADS):
            cols = slice(h * HGRN_DK, (h + 1) * HGRN_DK)
            a_col = jnp.sum(jnp.where(pick, at_ref[cols, :], 0.0), axis=1, keepdims=True)
            k_col = jnp.sum(jnp.where(pick, kt_ref[cols, :], 0.0), axis=1, keepdims=True)
            q_col = jnp.sum(jnp.where(pick, qt_ref[cols, :], 0.0), axis=1, keepdims=True)
            s_new = a_col * s_ref[j, h] + k_col * v_grp[j:j + 1, cols]
            s_out_ref[j, h] = s_new
            o_heads.append(jnp.sum(q_col * s_new, axis=0, keepdims=True))
        o_rows.append(jnp.concatenate(o_heads, axis=-1))
    o_ref[grp, :] = jnp.concatenate(o_rows, axis=0)

    @pl.when(i == n_g - 1)
    def _():
        gn = gn_ref[...]
        o_heads = []
        for h in range(HGRN_HEADS):
            o_h = o_ref[:, h * HGRN_DK:(h + 1) * HGRN_DK]
            o_heads.append(o_h * lax.rsqrt(jnp.mean(o_h * o_h, axis=-1, keepdims=True) + RMS_EPS) * gn)
        o = jnp.concatenate(o_heads, axis=-1) * gate_ref[...]
        cc = _silu(_layernorm(cacc_ref[...], clw_ref[...], clb_ref[...]))
        y = _dot(jnp.concatenate([o, cc], axis=-1).astype(BF16), w_out_ref[...])
        y_ref[...] = _layernorm(ALPHA * x_ref[...] + y, lnw_ref[...], lnb_ref[...])


def _even_sample(x, e, layer, w_in, w_out, logits, gn, dww, dwb, clw, clb, lnw, lnb, state_hgrn, state_conf,
                 prev_h=None, prev_c=None):
    n_seq, d = x.shape
    g = SAMPLE_GROUP
    assert n_seq % g == 0 and n_seq % 128 == 0
    n_g = n_seq // g
    aliased = prev_h is not None
    const2 = lambda i: (0, 0)
    in_specs = [
        pl.BlockSpec((n_seq, d), const2),
        pl.BlockSpec((None, d, EVEN_IN), lambda i: (e, 0, 0)),
        pl.BlockSpec((None, d, d), lambda i: (e, 0, 0)),
        pl.BlockSpec(logits.shape, const2),
        pl.BlockSpec((None, 1, HGRN_DV), lambda i: (e, 0, 0)),
        pl.BlockSpec((None, CONF_KERNEL, CONF_WIDTH), lambda i: (e, 0, 0)),
        pl.BlockSpec((None, 1, CONF_WIDTH), lambda i: (e, 0, 0)),
        pl.BlockSpec((None, 1, CONF_WIDTH), lambda i: (e, 0, 0)),
        pl.BlockSpec((None, 1, CONF_WIDTH), lambda i: (e, 0, 0)),
        pl.BlockSpec((None, 1, d), lambda i: (layer, 0, 0)),
        pl.BlockSpec((None, 1, d), lambda i: (layer, 0, 0)),
        pl.BlockSpec((None, g, HGRN_HEADS, HGRN_DK, HGRN_DV), lambda i: (e, i, 0, 0, 0)),
        pl.BlockSpec((None, CONF_KERNEL - 1, g, CONF_WIDTH), lambda i: (e, 0, i, 0)),
    ]
    args =[x, w_in, w_out, logits, gn, dww, dwb, clw, clb, lnw, lnb, state_hgrn, state_conf]
    aliases = {}
    if aliased:
        in_specs += [pl.BlockSpec(memory_space=pl.ANY), pl.BlockSpec(memory_space=pl.ANY)]
        aliases = {len(args): 1, len(args) + 1: 2}
        args += [prev_h, prev_c]
    return pl.pallas_call(
        functools.partial(_even_sample_kernel, e, n_g, aliased),
        grid=(n_g,),
        in_specs=in_specs,
        out_specs=[
            pl.BlockSpec((n_seq, d), const2),
            pl.BlockSpec((None, g, HGRN_HEADS, HGRN_DK, HGRN_DV), lambda i: (e, i, 0, 0, 0)),
            pl.BlockSpec((None, CONF_KERNEL - 1, g, CONF_WIDTH), lambda i: (e, 0, i, 0)),
        ],
        out_shape=[
            jax.ShapeDtypeStruct((n_seq, d), F32),
            jax.ShapeDtypeStruct(state_hgrn.shape, F32),
            jax.ShapeDtypeStruct(state_conf.shape, F32),
        ],
        scratch_shapes=[
            pltpu.VMEM((HGRN_WIDTH, n_seq), F32),
            pltpu.VMEM((HGRN_WIDTH, n_seq), F32),
            pltpu.VMEM((HGRN_WIDTH, n_seq), F32),
            pltpu.VMEM((n_seq, HGRN_WIDTH), F32),
            pltpu.VMEM((n_seq, HGRN_WIDTH), F32),
            pltpu.VMEM((n_seq, CONF_WIDTH), F32),
            pltpu.VMEM((n_seq, HGRN_WIDTH), F32),
            pltpu.VMEM((n_seq, CONF_WIDTH), F32),
        ],
        input_output_aliases=aliases,
        compiler_params=pltpu.CompilerParams(dimension_semantics=("arbitrary",), vmem_limit_bytes=VMEM_LIMIT),
        name=f"even_sample_{e}",
    )(*args)


def _odd_prompt_kernel(n_t, x_ref, w_in_ref, w_out_ref, cw_ref, lnw_ref, lnb_ref, y_ref, z_out_ref, zbuf_ref):
    t_idx = pl.program_id(1)
    tb = x_ref.shape[0]
    w = SC_WIDTH

    @pl.when(t_idx == 0)
    def _():
        _zero_history(zbuf_ref, SC_HIST)

    x = x_ref[...]
    xb = x.astype(BF16)
    z = _dot(xb, w_in_ref[:, w:2 * w]) * _dot(xb, w_in_ref[:, 2 * w:3 * w])
    conv, hist = _causal_dwconv(zbuf_ref, z, cw_ref, SC_HIST)
    gated = (_dot(xb, w_in_ref[:, 0:w]) * conv).astype(BF16)
    n_groups = 2 if tb % (2 * SUBLANES) == 0 else 1
    m = tb // n_groups
    for r in range(n_groups):
        rows = slice(r * m, (r + 1) * m)
        y_ref[rows, :] = _layernorm(ALPHA * x[rows] + _dot(gated[rows], w_out_ref[...]), lnw_ref[...], lnb_ref[...])

    @pl.when(t_idx == n_t - 1)
    def _():
        z_out_ref[...] = hist[SC_HIST - (SC_KERNEL - 1):, :]


def _odd_prompt(x, o, layer, w_in, w_out, cw, lnw, lnb):
    bsz, t_len, d = x.shape
    tb = min(TB_ODD, t_len)
    assert t_len % tb == 0 and t_len >= SC_KERNEL - 1
    n_t = t_len // tb
    return pl.pallas_call(
        functools.partial(_odd_prompt_kernel, n_t),
        grid=(bsz, n_t),
        in_specs=[
            pl.BlockSpec((None, tb, d), lambda b, t: (b, t, 0)),
            pl.BlockSpec((None, d, 3 * SC_WIDTH), lambda b, t: (o, 0, 0)),
            pl.BlockSpec((None, SC_WIDTH, d), lambda b, t: (o, 0, 0)),
            pl.BlockSpec((None, SC_KERNEL, SC_WIDTH), lambda b, t: (o, 0, 0)),
            pl.BlockSpec((None, 1, d), lambda b, t: (layer, 0, 0)),
            pl.BlockSpec((None, 1, d), lambda b, t: (layer, 0, 0)),
        ],
        out_specs=[
            pl.BlockSpec((None, tb, d), lambda b, t: (b, t, 0)),
            pl.BlockSpec((None, SC_KERNEL - 1, SC_WIDTH), lambda b, t: (b, 0, 0)),
        ],
        out_shape=[
            jax.ShapeDtypeStruct((bsz, t_len, d), F32),
            jax.ShapeDtypeStruct((bsz, SC_KERNEL - 1, SC_WIDTH), F32),
        ],
        scratch_shapes=[pltpu.VMEM((SC_WIDTH // LANES, ROW_PITCH * (SC_HIST + tb), LANES), F32)],
        compiler_params=pltpu.CompilerParams(dimension_semantics=("arbitrary", "arbitrary"),
                                             vmem_limit_bytes=VMEM_LIMIT),
        name=f"odd_prompt_{o}",
    )(x, w_in, w_out, cw, lnw, lnb)


def _odd_sample_kernel(x_ref, w_in_ref, w_out_ref, cw_ref, lnw_ref, lnb_ref, s_ref, y_ref, s_out_ref):
    w = SC_WIDTH
    x = x_ref[...]
    p = _dot(x.astype(BF16), w_in_ref[...])
    z = p[:, w:2 * w] * p[:, 2 * w:3 * w]
    s0 = s_ref[:, 0:w]
    s1 = s_ref[:, w:2 * w]
    conv = cw_ref[0:1, :] * s0 + cw_ref[1:2, :] * s1 + cw_ref[2:3, :] * z
    s_out_ref[:, 0:w] = s1
    s_out_ref[:, w:2 * w] = z
    y = _dot((p[:, 0:w] * conv).astype(BF16), w_out_ref[...])
    y_ref[...] = _layernorm(ALPHA * x + y, lnw_ref[...], lnb_ref[...])


def _odd_sample(x, o, layer, w_in, w_out, cw, lnw, lnb, state):
    n_seq, d = x.shape
    assert SC_KERNEL == 3
    hist = (SC_KERNEL - 1) * SC_WIDTH
    return pl.pallas_call(
        _odd_sample_kernel,
        grid=(1,),
        in_specs=[
            pl.BlockSpec((n_seq, d), lambda i: (0, 0)),
            pl.BlockSpec((None, d, 3 * SC_WIDTH), lambda i: (o, 0, 0)),
            pl.BlockSpec((None, SC_WIDTH, d), lambda i: (o, 0, 0)),
            pl.BlockSpec((None, SC_KERNEL, SC_WIDTH), lambda i: (o, 0, 0)),
            pl.BlockSpec((None, 1, d), lambda i: (layer, 0, 0)),
            pl.BlockSpec((None, 1, d), lambda i: (layer, 0, 0)),
            pl.BlockSpec((None, n_seq, hist), lambda i: (o, 0, 0)),
        ],
        out_specs=[
            pl.BlockSpec((n_seq, d), lambda i: (0, 0)),
            pl.BlockSpec((n_seq, hist), lambda i: (0, 0)),
        ],
        out_shape=[
            jax.ShapeDtypeStruct((n_seq, d), F32),
            jax.ShapeDtypeStruct((n_seq, hist), F32),
        ],
        compiler_params=pltpu.CompilerParams(dimension_semantics=("arbitrary",), vmem_limit_bytes=VMEM_LIMIT),
        name=f"odd_sample_{o}",
    )(x, w_in, w_out, cw, lnw, lnb, state)


def _ffn_kernel(n_p, xp_ref, xs_ref, w1_ref, w3_ref, w2_ref, lnw_ref, lnb_ref, yp_ref, ys_ref, pre_ref):
    i = pl.program_id(0)

    def residual_swiglu(x):
        xb = x.astype(BF16)
        h1 = _dot(xb, w1_ref[...])
        h3 = _dot(xb, w3_ref[...])
        return ALPHA * x + _dot((_silu(h1) * h3).astype(BF16), w2_ref[...])

    @pl.when(i == 0)
    def _():
        pre_ref[...] = jnp.zeros_like(pre_ref)

    @pl.when(i < n_p)
    def _():
        yp_ref[...] = _layernorm(pre_ref[...], lnw_ref[...], lnb_ref[...])
        pre_ref[...] = residual_swiglu(xp_ref[...])

    @pl.when(i == n_p)
    def _():
        yp_ref[...] = _layernorm(pre_ref[...], lnw_ref[...], lnb_ref[...])
        ys_ref[...] = _layernorm(residual_swiglu(xs_ref[...]), lnw_ref[...], lnb_ref[...])


def _ffn(xp, xs, layer, w1, w3, w2, lnw, lnb):
    m, d = xp.shape
    tm = min(TM_FFN, m)
    assert m % tm == 0
    n_p = m // tm
    prompt_block = pl.BlockSpec((tm, d), lambda i: (jnp.minimum(i, n_p - 1), 0))
    sample_block = pl.BlockSpec(xs.shape, lambda i: (0, 0))
    return pl.pallas_call(
        functools.partial(_ffn_kernel, n_p),
        grid=(n_p + 1,),
        in_specs=[
            prompt_block,
            sample_block,
            pl.BlockSpec((None, d, D_FF), lambda i: (layer, 0, 0)),
            pl.BlockSpec((None, d, D_FF), lambda i: (layer, 0, 0)),
            pl.BlockSpec((None, D_FF, d), lambda i: (layer, 0, 0)),
            pl.BlockSpec((None, 1, d), lambda i: (layer, 0, 0)),
            pl.BlockSpec((None, 1, d), lambda i: (layer, 0, 0)),
        ],
        out_specs=[pl.BlockSpec((tm, d), lambda i: (jnp.maximum(i - 1, 0), 0)), sample_block],
        out_shape=[jax.ShapeDtypeStruct((m, d), F32), jax.ShapeDtypeStruct(xs.shape, F32)],
        scratch_shapes=[pltpu.VMEM((tm, d), F32)],
        compiler_params=pltpu.CompilerParams(dimension_semantics=("arbitrary",), vmem_limit_bytes=VMEM_LIMIT),
        name=f"ffn_{layer}",
    )(xp, xs, w1, w3, w2, lnw, lnb)


def kernel(x_prompt, x_sample, state_hgrn, state_conf, state_sconv, w_in_even, w_out_even, hgrn_lb_logits,
           hgrn_gnorm_w, conf_dw_w, conf_dw_b, conf_ln_w, conf_ln_b, sc_w_in, sc_conv_w, sc_w_out,
           ffn_w1, ffn_w3, ffn_w2, ln_mix_w, ln_mix_b, ln_ffn_w, ln_ffn_b):
    bsz, t_len, d = x_prompt.shape
    n_seq = x_sample.shape[0]
    n_even, n_odd = state_hgrn.shape[0], state_sconv.shape[0]
    assert d == D_MODEL and x_sample.shape[1] == 1

    w_in_even = w_in_even.astype(BF16)
    w_out_even = w_out_even.astype(BF16)
    sc_w_in = sc_w_in.astype(BF16)
    sc_w_out = sc_w_out.astype(BF16)
    ffn_w1 = ffn_w1.astype(BF16)
    ffn_w3 = ffn_w3.astype(BF16)
    ffn_w2 = ffn_w2.astype(BF16)
    row = lambda a: a.reshape(a.shape[0], 1, a.shape[1])
    gn, dwb, clw, clb = row(hgrn_gnorm_w), row(conf_dw_b), row(conf_ln_w), row(conf_ln_b)
    lmw, lmb, lfw, lfb = row(ln_mix_w), row(ln_mix_b), row(ln_ffn_w), row(ln_ffn_b)
    sconv_flat = state_sconv.reshape(n_odd, n_seq, (SC_KERNEL - 1) * SC_WIDTH)
    conf_tmajor = jnp.transpose(state_conf, (0, 2, 1, 3))
    dall, lvl = _chunk_constants()

    xp = x_prompt
    xs = x_sample.reshape(n_seq, d)
    h_p, c_p, s_p, s_s = [], [], [], []
    h_s = c_s = None
    for layer in range(DEPTH):
        if layer % 2 == 0:
            e = layer // 2
            xp, sh, cb = _even_prompt(xp, e, layer, w_in_even, w_out_even, hgrn_lb_logits, gn, conf_dw_w, dwb,
                                      clw, clb, lmw, lmb, dall, lvl)
            h_p.append(sh)
            c_p.append(cb)
            xs, h_s, c_s = _even_sample(xs, e, layer, w_in_even, w_out_even, hgrn_lb_logits, gn, conf_dw_w, dwb,
                                        clw, clb, lmw, lmb, state_hgrn, conf_tmajor, h_s, c_s)
        else:
            o = layer // 2
            xp, sb = _odd_prompt(xp, o, layer, sc_w_in, sc_w_out, sc_conv_w, lmw, lmb)
            s_p.append(sb)
            xs, sb = _odd_sample(xs, o, layer, sc_w_in, sc_w_out, sc_conv_w, lmw, lmb, sconv_flat)
            s_s.append(sb.reshape(n_seq, SC_KERNEL - 1, SC_WIDTH))
        xp, xs = _ffn(xp.reshape(bsz * t_len, d), xs, layer, ffn_w1, ffn_w3, ffn_w2, lfw, lfb)
        xp = xp.reshape(bsz, t_len, d)
    return (xp, xs.reshape(n_seq, 1, d), jnp.stack(h_p), jnp.stack(c_p), jnp.stack(s_p), h_s,
            jnp.transpose(c_s, (0, 2, 1, 3)), jnp.stack(s_s))
```

```python
import functools
import math

import numpy as np
import jax
import jax.numpy as jnp
from jax import lax
from jax.experimental import pallas as pl
from jax.experimental.pallas import tpu as pltpu

F32 = jnp.float32
BF16 = jnp.bfloat16

D_MODEL = 1024
DEPTH = 4
HGRN_WIDTH = D_MODEL // 2
HGRN_DK = 128
HGRN_DV = 128
HGRN_HEADS = HGRN_WIDTH // HGRN_DK
CONF_WIDTH = D_MODEL - HGRN_WIDTH
CONF_KERNEL = 31
SC_WIDTH = D_MODEL
SC_KERNEL = 3
D_FF = 256 * math.ceil(8 * D_MODEL / 3 / 256)
EVEN_IN = 4 * HGRN_WIDTH + 2 * CONF_WIDTH
ALPHA = (2 * DEPTH) ** 0.25
LOG2_E = math.log2(math.e)
LN_EPS = 1e-5
RMS_EPS = 1e-6

SUBLANES = 8
LANES = 128
CHUNK = 128
N_LEVELS = CHUNK.bit_length() - 1
N_MXU_LEVELS = SUBLANES.bit_length() - 1
TB_EVEN = 512
TB_ODD = 512
TM_FFN = 512
CONF_HIST = 32
SC_HIST = SUBLANES
ROW_PITCH = 3
SAMPLE_GROUP = 8
VMEM_LIMIT = 56 * 1024 * 1024


def _sigmoid(x):
    return 1.0 / (1.0 + jnp.exp(-x))


def _silu(x):
    return x * _sigmoid(x)


def _layernorm(x, w, b):
    mu = jnp.mean(x, axis=-1, keepdims=True)
    xc = x - mu
    var = jnp.mean(xc * xc, axis=-1, keepdims=True)
    return xc * lax.rsqrt(var + LN_EPS) * w + b


def _dot(a, b):
    return jnp.dot(a, b, preferred_element_type=F32)


def _dot_nt(a, b):
    return lax.dot_general(a, b, (((1,), (1,)), ((), ())), preferred_element_type=F32)


def _dot_tn(a, b):
    return lax.dot_general(a, b, (((0,), (0,)), ((), ())), preferred_element_type=F32)


def _split3(x):
    hi = x.astype(BF16)
    r1 = x - hi.astype(F32)
    mid = r1.astype(BF16)
    lo = (r1 - mid.astype(F32)).astype(BF16)
    return hi, mid, lo


def _lower_bound(logits, e):
    m = jnp.max(logits, axis=0, keepdims=True)
    ex = jnp.exp(logits - m)
    sm = ex / jnp.sum(ex, axis=0, keepdims=True)
    lb = jnp.zeros_like(m)
    for i in range(1, e + 1):
        lb = lb + sm[i:i + 1]
    return lb


def _hgrn_gates(zq, zf, lb):
    t = jnp.exp(-jnp.abs(zf))
    s = 1.0 + t
    r = 1.0 / s
    sig_neg = jnp.where(zf >= 0.0, t * r, r)
    log_sig = jnp.minimum(zf, 0.0) - jnp.log(s)
    a = jnp.log(lb)
    c = jnp.log1p(-lb) + log_sig
    logf = jnp.maximum(a, c) + jnp.log(1.0 + jnp.exp(-jnp.abs(a - c)))
    kk = (1.0 - lb) * sig_neg
    qq = _silu(zq) * (HGRN_DK ** -0.5)
    return qq, kk, logf


def _level_operand(q, k, e_j, j):
    L, width = q.shape
    half = 1 << (j - 1)
    if half >= SUBLANES:
        pieces = []
        for s in range(0, L, 2 * half):
            pieces.append(k[s:s + half])
            pieces.append(q[s + half:s + 2 * half])
        base = jnp.concatenate(pieces, axis=0)
    else:
        sub = lax.broadcasted_iota(jnp.int32, (1, SUBLANES, width), 1)
        upper = ((sub >> (j - 1)) & 1) == 1
        shape3 = (L // SUBLANES, SUBLANES, width)
        base = jnp.where(upper, q.reshape(shape3), k.reshape(shape3)).reshape(L, width)
    return (base * e_j).astype(BF16)


def _block_diag(a):
    z = jnp.zeros((a.shape[0], LANES), a.dtype)
    return jnp.concatenate([jnp.concatenate([a[:, :LANES], z], axis=1),
                            jnp.concatenate([z, a[:, LANES:]], axis=1)], axis=0)


def _chunk_constants():
    L = CHUNK
    t = np.arange(L)[:, None]
    r = np.arange(L)[None, :]
    mats = []
    for j in range(1, N_MXU_LEVELS + 1):
        half = 1 << (j - 1)
        start = (t >> j) << j
        mid = start + half - 1
        upper = t >= start + half
        d = np.where(upper, (r > mid) & (r <= t), (r > t) & (r <= mid))
        mats.append(d)
    mats.append(r <= t)
    d_all = np.concatenate(mats, axis=0).astype(np.float32)
    x = np.bitwise_xor(t, r)
    lvl = np.zeros((L, L), np.int32)
    nz = x > 0
    lvl[nz] = np.floor(np.log2(x[nz])).astype(np.int32) + 1
    lvl = np.where(r > t, -1, lvl).astype(np.int32)
    d_all = np.concatenate([d_all] * 3, axis=1)
    lvl = np.concatenate([lvl, lvl], axis=1)
    return jnp.asarray(d_all, dtype=BF16), jnp.asarray(lvl)


def _pitched(start, n):
    return pl.ds(ROW_PITCH * start, n, stride=ROW_PITCH)


def _zero_history(buf_ref, n_hist):
    for c in range(buf_ref.shape[0]):
        buf_ref[c, _pitched(0, n_hist), :] = jnp.zeros((n_hist, LANES), F32)


def _causal_dwconv(buf_ref, new_rows, w_ref, n_hist):
    tiles = [_causal_dwconv_tile(buf_ref, c, new_rows, w_ref, n_hist) for c in range(new_rows.shape[1] // LANES)]
    return jnp.concatenate([t[0] for t in tiles], axis=-1), jnp.concatenate([t[1] for t in tiles], axis=-1)


def _causal_dwconv_tile(buf_ref, c, new_rows, w_ref, n_hist):
    tb = new_rows.shape[0]
    taps = w_ref.shape[0]
    off = n_hist - (taps - 1)
    cols = slice(c * LANES, (c + 1) * LANES)
    buf_ref[c, _pitched(n_hist, tb), :] = new_rows[:, cols]
    acc = w_ref[taps - 1:taps, cols] * new_rows[:, cols]
    for j in range(taps - 1):
        acc = acc + w_ref[j:j + 1, cols] * buf_ref[c, _pitched(off + j, tb), :]
    hist = buf_ref[c, _pitched(tb, n_hist), :]
    buf_ref[c, _pitched(0, n_hist), :] = hist
    return acc, hist


def _even_prompt_kernel(e, n_t, x_ref, w_in_ref, w_out_ref, logit_ref, gn_ref, dww_ref, dwb_ref, clw_ref, clb_ref,
                        lnw_ref, lnb_ref, dall_ref, lvl_ref, y_ref, s_out_ref, conf_out_ref, st_ref, ubuf_ref):
    t_idx = pl.program_id(1)
    tb = x_ref.shape[0]
    hw = HGRN_WIDTH

    @pl.when(t_idx == 0)
    def _():
        st_ref[...] = jnp.zeros_like(st_ref)
        _zero_history(ubuf_ref, CONF_HIST)

    x = x_ref[...]
    xb = x.astype(BF16)
    L = CHUNK
    n_tiles = CONF_WIDTH // LANES

    pc = _dot(xb, w_in_ref[:, 4 * hw:])
    u = pc[:, :CONF_WIDTH] * _sigmoid(pc[:, CONF_WIDTH:])
    conv_tiles = [_causal_dwconv_tile(ubuf_ref, 0, u, dww_ref, CONF_HIST)]
    pq = _dot(xb, w_in_ref[:, 0:2 * hw])
    conv_tiles.append(_causal_dwconv_tile(ubuf_ref, 1, u, dww_ref, CONF_HIST))
    pv = _dot(xb, w_in_ref[:, 2 * hw:4 * hw])
    vi = pv[:, 0:hw].astype(BF16)
    zg = pv[:, hw:]

    def conv_finish():
        conv = jnp.concatenate([t[0] for t in conv_tiles], axis=-1)
        cc = _silu(_layernorm(conv + dwb_ref[...], clw_ref[...], clb_ref[...]))
        return _dot(cc.astype(BF16), w_out_ref[hw:, :])

    fill = [lambda c=c: conv_tiles.append(_causal_dwconv_tile(ubuf_ref, c, u, dww_ref, CONF_HIST))
            for c in range(2, n_tiles)] + [conv_finish]
    fill_out = []

    lb = _lower_bound(logit_ref[...], e)
    lvl = lvl_ref[...]
    dall = dall_ref[...]
    gn = gn_ref[...]
    n_pairs = HGRN_HEADS // 2

    def exponents(c):
        rows = slice(c * L, (c + 1) * L)
        qq, kk, logf = _hgrn_gates(pq[rows, 0:hw], pq[rows, hw:], lb)
        g3 = jnp.concatenate(_split3(logf * LOG2_E), axis=0)
        return qq, kk, _dot(dall, g3)

    def chunk(c, qq, kk, eb):
        rows = slice(c * L, (c + 1) * L)
        b = eb[N_MXU_LEVELS * L:, :]
        e_lvl = [jnp.exp2(eb[(j - 1) * L:j * L, :]) for j in range(1, N_MXU_LEVELS + 1)]
        for j in range(N_MXU_LEVELS + 1, N_LEVELS + 1):
            size, half = 1 << j, 1 << (j - 1)
            pieces = []
            for s in range(0, L, size):
                b_m = b[s + half - 1:s + half, :]
                pieces += [b_m - b[s:s + half, :], b[s + half:s + size, :] - b_m]
            e_lvl.append(jnp.exp2(jnp.concatenate(pieces, axis=0)))
        e_q_all = jnp.exp2(b)
        e_k_all = jnp.exp2(b[L - 1:L, :] - b)
        cols = [slice(2 * hp * HGRN_DK, 2 * (hp + 1) * HGRN_DK) for hp in range(n_pairs)]
        q2 = [qq[:, cs] for cs in cols]
        k2 = [kk[:, cs] for cs in cols]
        lvl_v = jnp.concatenate([lvl[:, :L], lvl[:, :L]], axis=0)
        scores = [jnp.where(lvl_v == 0, _dot_nt(_block_diag(q2[hp].astype(BF16)), k2[hp].astype(BF16)), 0.0)
                  for hp in range(n_pairs)]
        for j in range(1, N_LEVELS + 1):
            for hp in range(n_pairs):
                m_j = _level_operand(q2[hp], k2[hp], e_lvl[j - 1][:, cols[hp]], j)
                scores[hp] = jnp.where(lvl_v == j, _dot_nt(_block_diag(m_j), m_j), scores[hp])
        scores = [jnp.concatenate([sc[:L], sc[L:]], axis=1) for sc in scores]
        o_heads = []
        for hp in range(n_pairs):
            v2 = vi[rows, cols[hp]]
            e_q = e_q_all[:, cols[hp]]
            qd = (q2[hp] * e_q).astype(BF16)
            kd = (k2[hp] * e_k_all[:, cols[hp]]).astype(BF16)
            st = [st_ref[2 * hp], st_ref[2 * hp + 1]]
            st2 = jnp.concatenate([st[0].astype(BF16), st[1].astype(BF16)], axis=1)
            o2 = _dot(scores[hp].astype(BF16), _block_diag(v2)) + _dot_nt(qd, _block_diag(st2))
            for i in range(2):
                hc = slice(i * HGRN_DK, (i + 1) * HGRN_DK)
                st_ref[2 * hp + i] = e_q[L - 1:L, hc] * st[i] + _dot_tn(v2[:, hc], kd[:, hc])
                o_h = o2[:, hc]
                o_heads.append(o_h * lax.rsqrt(jnp.mean(o_h * o_h, axis=-1, keepdims=True) + RMS_EPS) * gn)
        return jnp.concatenate(o_heads, axis=-1) * _silu(zg[rows])

    n_chunks = tb // L
    o_chunks = []
    finished = 0

    def finish(upto, y_conv):
        rows = slice(finished * L, upto * L)
        o_rows = jnp.concatenate(o_chunks[finished:upto], axis=0)
        y = y_conv[rows] + _dot(o_rows.astype(BF16), w_out_ref[0:hw, :])
        y_ref[rows, :] = _layernorm(ALPHA * x[rows] + y, lnw_ref[...], lnb_ref[...])

    ahead = exponents(0)
    for c in range(n_chunks):
        cur = ahead
        if c + 1 < n_chunks:
            ahead = exponents(c + 1)
        o_chunks.append(chunk(c, *cur))
        if fill:
            fill_out.append(fill.pop(0)())
        if len(fill) == 1 and c + 1 < n_chunks:
            fill_out.append(fill.pop(0)())
        if not fill and c + 1 == n_chunks // 2 + 1 and n_chunks >= 4:
            finish(n_chunks // 2, fill_out[-1])
            finished = n_chunks // 2
    while fill:
        fill_out.append(fill.pop(0)())
    finish(n_chunks, fill_out[-1])

    @pl.when(t_idx == n_t - 1)
    def _():
        hist = jnp.concatenate([t[1] for t in conv_tiles], axis=-1)
        conf_out_ref[...] = hist[CONF_HIST - (CONF_KERNEL - 1):, :]
        for h in range(HGRN_HEADS):
            s_out_ref[h] = st_ref[h].T


def _even_prompt(x, e, layer, w_in, w_out, logits, gn, dww, dwb, clw, clb, lnw, lnb, dall, lvl):
    bsz, t_len, d = x.shape
    tb = min(TB_EVEN, t_len)
    assert t_len % tb == 0 and tb % CHUNK == 0 and t_len >= CONF_KERNEL - 1
    n_t = t_len // tb
    const2 = lambda b, t: (0, 0)
    return pl.pallas_call(
        functools.partial(_even_prompt_kernel, e, n_t),
        grid=(bsz, n_t),
        in_specs=[
            pl.BlockSpec((None, tb, d), lambda b, t: (b, t, 0)),
            pl.BlockSpec((None, d, EVEN_IN), lambda b, t: (e, 0, 0)),
            pl.BlockSpec((None, d, d), lambda b, t: (e, 0, 0)),
            pl.BlockSpec(logits.shape, const2),
            pl.BlockSpec((None, 1, HGRN_DV), lambda b, t: (e, 0, 0)),
            pl.BlockSpec((None, CONF_KERNEL, CONF_WIDTH), lambda b, t: (e, 0, 0)),
            pl.BlockSpec((None, 1, CONF_WIDTH), lambda b, t: (e, 0, 0)),
            pl.BlockSpec((None, 1, CONF_WIDTH), lambda b, t: (e, 0, 0)),
            pl.BlockSpec((None, 1, CONF_WIDTH), lambda b, t: (e, 0, 0)),
            pl.BlockSpec((None, 1, d), lambda b, t: (layer, 0, 0)),
            pl.BlockSpec((None, 1, d), lambda b, t: (layer, 0, 0)),
            pl.BlockSpec(dall.shape, const2),
            pl.BlockSpec(lvl.shape, const2),
        ],
        out_specs=[
            pl.BlockSpec((None, tb, d), lambda b, t: (b, t, 0)),
            pl.BlockSpec((None, HGRN_HEADS, HGRN_DK, HGRN_DV), lambda b, t: (b, 0, 0, 0)),
            pl.BlockSpec((None, CONF_KERNEL - 1, CONF_WIDTH), lambda b, t: (b, 0, 0)),
        ],
        out_shape=[
            jax.ShapeDtypeStruct((bsz, t_len, d), F32),
            jax.ShapeDtypeStruct((bsz, HGRN_HEADS, HGRN_DK, HGRN_DV), F32),
            jax.ShapeDtypeStruct((bsz, CONF_KERNEL - 1, CONF_WIDTH), F32),
        ],
        scratch_shapes=[
            pltpu.VMEM((HGRN_HEADS, HGRN_DV, HGRN_DK), F32),
            pltpu.VMEM((CONF_WIDTH // LANES, ROW_PITCH * (CONF_HIST + tb), LANES), F32),
        ],
        compiler_params=pltpu.CompilerParams(dimension_semantics=("arbitrary", "arbitrary"),
                                             vmem_limit_bytes=VMEM_LIMIT),
        name=f"even_prompt_{e}",
    )(x, w_in, w_out, logits, gn, dww, dwb, clw, clb, lnw, lnb, dall, lvl)


def _even_sample_kernel(e, n_g, aliased, x_ref, w_in_ref, w_out_ref, logit_ref, gn_ref, dww_ref, dwb_ref, clw_ref,
                        clb_ref, lnw_ref, lnb_ref, s_ref, c_ref, *rest):
    if aliased:
        rest = rest[2:]
    y_ref, s_out_ref, c_out_ref, qt_ref, at_ref, kt_ref, v_ref, gate_ref, u_ref, o_ref, cacc_ref = rest
    i = pl.program_id(0)
    hw = HGRN_WIDTH
    n_seq = x_ref.shape[0]

    @pl.when(i == 0)
    def _():
        p = _dot(x_ref[...].astype(BF16), w_in_ref[...])
        lb = _lower_bound(logit_ref[...], e)
        qq, kk, logf = _hgrn_gates(p[:, 0:hw], p[:, hw:2 * hw], lb)
        qt_ref[...] = qq.T
        kt_ref[...] = kk.T
        at_ref[...] = jnp.exp(logf).T
        v_ref[...] = p[:, 2 * hw:3 * hw]
        gate_ref[...] = _silu(p[:, 3 * hw:4 * hw])
        u_ref[...] = p[:, 4 * hw:4 * hw + CONF_WIDTH] * _sigmoid(p[:, 4 * hw + CONF_WIDTH:])

    lane = lax.broadcasted_iota(jnp.int32, (HGRN_DK, n_seq), 1)
    n_hist = CONF_KERNEL - 1
    grp = pl.ds(pl.multiple_of(i * SAMPLE_GROUP, SAMPLE_GROUP), SAMPLE_GROUP)
    v_grp = v_ref[grp, :]
    u_grp = u_ref[grp, :]

    conv = dww_ref[n_hist:n_hist + 1, :] * u_grp + dwb_ref[...]
    for j in range(n_hist):
        conv = conv + dww_ref[j:j + 1, :] * c_ref[j]
    cacc_ref[grp, :] = conv
    c_out_ref[0:n_hist - 1] = c_ref[1:n_hist]
    c_out_ref[n_hist - 1] = u_grp

    o_rows = []
    for j in range(SAMPLE_GROUP):
        pick = lane == i * SAMPLE_GROUP + j
        o_heads = []
        for h in range(HGRN_HEADS):
            cols = slice(h * HGRN_DK, (h + 1) * HGRN_DK)
            a_col = jnp.sum(jnp.where(pick, at_ref[cols, :], 0.0), axis=1, keepdims=True)
            k_col = jnp.sum(jnp.where(pick, kt_ref[cols, :], 0.0), axis=1, keepdims=True)
            q_col = jnp.sum(jnp.where(pick, qt_ref[cols, :], 0.0), axis=1, keepdims=True)
            s_new = a_col * s_ref[j, h] + k_col * v_grp[j:j + 1, cols]
            s_out_ref[j, h] = s_new
            o_heads.append(jnp.sum(q_col * s_new, axis=0, keepdims=True))
        o_rows.append(jnp.concatenate(o_heads, axis=-1))
    o_ref[grp, :] = jnp.concatenate(o_rows, axis=0)

    @pl.when(i == n_g - 1)
    def _():
        gn = gn_ref[...]
        o_heads = []
        for h in range(HGRN_HEADS):
            o_h = o_ref[:, h * HGRN_DK:(h + 1) * HGRN_DK]
            o_heads.append(o_h * lax.rsqrt(jnp.mean(o_h * o_h, axis=-1, keepdims=True) + RMS_EPS) * gn)
        o = jnp.concatenate(o_heads, axis=-1) * gate_ref[...]
        cc = _silu(_layernorm(cacc_ref[...], clw_ref[...], clb_ref[...]))
        y = _dot(jnp.concatenate([o, cc], axis=-1).astype(BF16), w_out_ref[...])
        y_ref[...] = _layernorm(ALPHA * x_ref[...] + y, lnw_ref[...], lnb_ref[...])


def _even_sample(x, e, layer, w_in, w_out, logits, gn, dww, dwb, clw, clb, lnw, lnb, state_hgrn, state_conf,
                 prev_h=None, prev_c=None):
    n_seq, d = x.shape
    g = SAMPLE_GROUP
    assert n_seq % g == 0 and n_seq % 128 == 0
    n_g = n_seq // g
    aliased = prev_h is not None
    const2 = lambda i: (0, 0)
    in_specs = [
        pl.BlockSpec((n_seq, d), const2),
        pl.BlockSpec((None, d, EVEN_IN), lambda i: (e, 0, 0)),
        pl.BlockSpec((None, d, d), lambda i: (e, 0, 0)),
        pl.BlockSpec(logits.shape, const2),
        pl.BlockSpec((None, 1, HGRN_DV), lambda i: (e, 0, 0)),
        pl.BlockSpec((None, CONF_KERNEL, CONF_WIDTH), lambda i: (e, 0, 0)),
        pl.BlockSpec((None, 1, CONF_WIDTH), lambda i: (e, 0, 0)),
        pl.BlockSpec((None, 1, CONF_WIDTH), lambda i: (e, 0, 0)),
        pl.BlockSpec((None, 1, CONF_WIDTH), lambda i: (e, 0, 0)),
        pl.BlockSpec((None, 1, d), lambda i: (layer, 0, 0)),
        pl.BlockSpec((None, 1, d), lambda i: (layer, 0, 0)),
        pl.BlockSpec((None, g, HGRN_HEADS, HGRN_DK, HGRN_DV), lambda i: (e, i, 0, 0, 0)),
        pl.BlockSpec((None, CONF_KERNEL - 1, g, CONF_WIDTH), lambda i: (e, 0, i, 0)),
    ]
    args =[x, w_in, w_out, logits, gn, dww, dwb, clw, clb, lnw, lnb, state_hgrn, state_conf]
    aliases = {}
    if aliased:
        in_specs += [pl.BlockSpec(memory_space=pl.ANY), pl.BlockSpec(memory_space=pl.ANY)]
        aliases = {len(args): 1, len(args) + 1: 2}
        args += [prev_h, prev_c]
    return pl.pallas_call(
        functools.partial(_even_sample_kernel, e, n_g, aliased),
        grid=(n_g,),
        in_specs=in_specs,
        out_specs=[
            pl.BlockSpec((n_seq, d), const2),
            pl.BlockSpec((None, g, HGRN_HEADS, HGRN_DK, HGRN_DV), lambda i: (e, i, 0, 0, 0)),
            pl.BlockSpec((None, CONF_KERNEL - 1, g, CONF_WIDTH), lambda i: (e, 0, i, 0)),
        ],
        out_shape=[
            jax.ShapeDtypeStruct((n_seq, d), F32),
            jax.ShapeDtypeStruct(state_hgrn.shape, F32),
            jax.ShapeDtypeStruct(state_conf.shape, F32),
        ],
        scratch_shapes=[
            pltpu.VMEM((HGRN_WIDTH, n_seq), F32),
            pltpu.VMEM((HGRN_WIDTH, n_seq), F32),
            pltpu.VMEM((HGRN_WIDTH, n_seq), F32),
            pltpu.VMEM((n_seq, HGRN_WIDTH), F32),
            pltpu.VMEM((n_seq, HGRN_WIDTH), F32),
            pltpu.VMEM((n_seq, CONF_WIDTH), F32),
            pltpu.VMEM((n_seq, HGRN_WIDTH), F32),
            pltpu.VMEM((n_seq, CONF_WIDTH), F32),
        ],
        input_output_aliases=aliases,
        compiler_params=pltpu.CompilerParams(dimension_semantics=("arbitrary",), vmem_limit_bytes=VMEM_LIMIT),
        name=f"even_sample_{e}",
    )(*args)


def _odd_prompt_kernel(n_t, x_ref, w_in_ref, w_out_ref, cw_ref, lnw_ref, lnb_ref, y_ref, z_out_ref, zbuf_ref):
    t_idx = pl.program_id(1)
    tb = x_ref.shape[0]
    w = SC_WIDTH

    @pl.when(t_idx == 0)
    def _():
        _zero_history(zbuf_ref, SC_HIST)

    x = x_ref[...]
    xb = x.astype(BF16)
    z = _dot(xb, w_in_ref[:, w:2 * w]) * _dot(xb, w_in_ref[:, 2 * w:3 * w])
    conv, hist = _causal_dwconv(zbuf_ref, z, cw_ref, SC_HIST)
    gated = (_dot(xb, w_in_ref[:, 0:w]) * conv).astype(BF16)
    n_groups = 2 if tb % (2 * SUBLANES) == 0 else 1
    m = tb // n_groups
    for r in range(n_groups):
        rows = slice(r * m, (r + 1) * m)
        y_ref[rows, :] = _layernorm(ALPHA * x[rows] + _dot(gated[rows], w_out_ref[...]), lnw_ref[...], lnb_ref[...])

    @pl.when(t_idx == n_t - 1)
    def _():
        z_out_ref[...] = hist[SC_HIST - (SC_KERNEL - 1):, :]


def _odd_prompt(x, o, layer, w_in, w_out, cw, lnw, lnb):
    bsz, t_len, d = x.shape
    tb = min(TB_ODD, t_len)
    assert t_len % tb == 0 and t_len >= SC_KERNEL - 1
    n_t = t_len // tb
    return pl.pallas_call(
        functools.partial(_odd_prompt_kernel, n_t),
        grid=(bsz, n_t),
        in_specs=[
            pl.BlockSpec((None, tb, d), lambda b, t: (b, t, 0)),
            pl.BlockSpec((None, d, 3 * SC_WIDTH), lambda b, t: (o, 0, 0)),
            pl.BlockSpec((None, SC_WIDTH, d), lambda b, t: (o, 0, 0)),
            pl.BlockSpec((None, SC_KERNEL, SC_WIDTH), lambda b, t: (o, 0, 0)),
            pl.BlockSpec((None, 1, d), lambda b, t: (layer, 0, 0)),
            pl.BlockSpec((None, 1, d), lambda b, t: (layer, 0, 0)),
        ],
        out_specs=[
            pl.BlockSpec((None, tb, d), lambda b, t: (b, t, 0)),
            pl.BlockSpec((None, SC_KERNEL - 1, SC_WIDTH), lambda b, t: (b, 0, 0)),
        ],
        out_shape=[
            jax.ShapeDtypeStruct((bsz, t_len, d), F32),
            jax.ShapeDtypeStruct((bsz, SC_KERNEL - 1, SC_WIDTH), F32),
        ],
        scratch_shapes=[pltpu.VMEM((SC_WIDTH // LANES, ROW_PITCH * (SC_HIST + tb), LANES), F32)],
        compiler_params=pltpu.CompilerParams(dimension_semantics=("arbitrary", "arbitrary"),
                                             vmem_limit_bytes=VMEM_LIMIT),
        name=f"odd_prompt_{o}",
    )(x, w_in, w_out, cw, lnw, lnb)


def _odd_sample_kernel(x_ref, w_in_ref, w_out_ref, cw_ref, lnw_ref, lnb_ref, s_ref, y_ref, s_out_ref):
    w = SC_WIDTH
    x = x_ref[...]
    p = _dot(x.astype(BF16), w_in_ref[...])
    z = p[:, w:2 * w] * p[:, 2 * w:3 * w]
    s0 = s_ref[:, 0:w]
    s1 = s_ref[:, w:2 * w]
    conv = cw_ref[0:1, :] * s0 + cw_ref[1:2, :] * s1 + cw_ref[2:3, :] * z
    s_out_ref[:, 0:w] = s1
    s_out_ref[:, w:2 * w] = z
    y = _dot((p[:, 0:w] * conv).astype(BF16), w_out_ref[...])
    y_ref[...] = _layernorm(ALPHA * x + y, lnw_ref[...], lnb_ref[...])


def _odd_sample(x, o, layer, w_in, w_out, cw, lnw, lnb, state):
    n_seq, d = x.shape
    assert SC_KERNEL == 3
    hist = (SC_KERNEL - 1) * SC_WIDTH
    return pl.pallas_call(
        _odd_sample_kernel,
        grid=(1,),
        in_specs=[
            pl.BlockSpec((n_seq, d), lambda i: (0, 0)),
            pl.BlockSpec((None, d, 3 * SC_WIDTH), lambda i: (o, 0, 0)),
            pl.BlockSpec((None, SC_WIDTH, d), lambda i: (o, 0, 0)),
            pl.BlockSpec((None, SC_KERNEL, SC_WIDTH), lambda i: (o, 0, 0)),
            pl.BlockSpec((None, 1, d), lambda i: (layer, 0, 0)),
            pl.BlockSpec((None, 1, d), lambda i: (layer, 0, 0)),
            pl.BlockSpec((None, n_seq, hist), lambda i: (o, 0, 0)),
        ],
        out_specs=[
            pl.BlockSpec((n_seq, d), lambda i: (0, 0)),
            pl.BlockSpec((n_seq, hist), lambda i: (0, 0)),
        ],
        out_shape=[
            jax.ShapeDtypeStruct((n_seq, d), F32),
            jax.ShapeDtypeStruct((n_seq, hist), F32),
        ],
        compiler_params=pltpu.CompilerParams(dimension_semantics=("arbitrary",), vmem_limit_bytes=VMEM_LIMIT),
        name=f"odd_sample_{o}",
    )(x, w_in, w_out, cw, lnw, lnb, state)


def _ffn_kernel(n_p, xp_ref, xs_ref, w1_ref, w3_ref, w2_ref, lnw_ref, lnb_ref, yp_ref, ys_ref, pre_ref):
    i = pl.program_id(0)

    def residual_swiglu(x):
        xb = x.astype(BF16)
        h1 = _dot(xb, w1_ref[...])
        h3 = _dot(xb, w3_ref[...])
        return ALPHA * x + _dot((_silu(h1) * h3).astype(BF16), w2_ref[...])

    @pl.when(i == 0)
    def _():
        pre_ref[...] = jnp.zeros_like(pre_ref)

    @pl.when(i < n_p)
    def _():
        yp_ref[...] = _layernorm(pre_ref[...], lnw_ref[...], lnb_ref[...])
        pre_ref[...] = residual_swiglu(xp_ref[...])

    @pl.when(i == n_p)
    def _():
        yp_ref[...] = _layernorm(pre_ref[...], lnw_ref[...], lnb_ref[...])
        ys_ref[...] = _layernorm(residual_swiglu(xs_ref[...]), lnw_ref[...], lnb_ref[...])


def _ffn(xp, xs, layer, w1, w3, w2, lnw, lnb):
    m, d = xp.shape
    tm = min(TM_FFN, m)
    assert m % tm == 0
    n_p = m // tm
    prompt_block = pl.BlockSpec((tm, d), lambda i: (jnp.minimum(i, n_p - 1), 0))
    sample_block = pl.BlockSpec(xs.shape, lambda i: (0, 0))
    return pl.pallas_call(
        functools.partial(_ffn_kernel, n_p),
        grid=(n_p + 1,),
        in_specs=[
            prompt_block,
            sample_block,
            pl.BlockSpec((None, d, D_FF), lambda i: (layer, 0, 0)),
            pl.BlockSpec((None, d, D_FF), lambda i: (layer, 0, 0)),
            pl.BlockSpec((None, D_FF, d), lambda i: (layer, 0, 0)),
            pl.BlockSpec((None, 1, d), lambda i: (layer, 0, 0)),
            pl.BlockSpec((None, 1, d), lambda i: (layer, 0, 0)),
        ],
        out_specs=[pl.BlockSpec((tm, d), lambda i: (jnp.maximum(i - 1, 0), 0)), sample_block],
        out_shape=[jax.ShapeDtypeStruct((m, d), F32), jax.ShapeDtypeStruct(xs.shape, F32)],
        scratch_shapes=[pltpu.VMEM((tm, d), F32)],
        compiler_params=pltpu.CompilerParams(dimension_semantics=("arbitrary",), vmem_limit_bytes=VMEM_LIMIT),
        name=f"ffn_{layer}",
    )(xp, xs, w1, w3, w2, lnw, lnb)


def kernel(x_prompt, x_sample, state_hgrn, state_conf, state_sconv, w_in_even, w_out_even, hgrn_lb_logits,
           hgrn_gnorm_w, conf_dw_w, conf_dw_b, conf_ln_w, conf_ln_b, sc_w_in, sc_conv_w, sc_w_out,
           ffn_w1, ffn_w3, ffn_w2, ln_mix_w, ln_mix_b, ln_ffn_w, ln_ffn_b):
    bsz, t_len, d = x_prompt.shape
    n_seq = x_sample.shape[0]
    n_even, n_odd = state_hgrn.shape[0], state_sconv.shape[0]
    assert d == D_MODEL and x_sample.shape[1] == 1

    w_in_even = w_in_even.astype(BF16)
    w_out_even = w_out_even.astype(BF16)
    sc_w_in = sc_w_in.astype(BF16)
    sc_w_out = sc_w_out.astype(BF16)
    ffn_w1 = ffn_w1.astype(BF16)
    ffn_w3 = ffn_w3.astype(BF16)
    ffn_w2 = ffn_w2.astype(BF16)
    row = lambda a: a.reshape(a.shape[0], 1, a.shape[1])
    gn, dwb, clw, clb = row(hgrn_gnorm_w), row(conf_dw_b), row(conf_ln_w), row(conf_ln_b)
    lmw, lmb, lfw, lfb = row(ln_mix_w), row(ln_mix_b), row(ln_ffn_w), row(ln_ffn_b)
    sconv_flat = state_sconv.reshape(n_odd, n_seq, (SC_KERNEL - 1) * SC_WIDTH)
    conf_tmajor = jnp.transpose(state_conf, (0, 2, 1, 3))
    dall, lvl = _chunk_constants()

    xp = x_prompt
    xs = x_sample.reshape(n_seq, d)
    h_p, c_p, s_p, s_s = [], [], [], []
    h_s = c_s = None
    for layer in range(DEPTH):
        if layer % 2 == 0:
            e = layer // 2
            xp, sh, cb = _even_prompt(xp, e, layer, w_in_even, w_out_even, hgrn_lb_logits, gn, conf_dw_w, dwb,
                                      clw, clb, lmw, lmb, dall, lvl)
            h_p.append(sh)
            c_p.append(cb)
            xs, h_s, c_s = _even_sample(xs, e, layer, w_in_even, w_out_even, hgrn_lb_logits, gn, conf_dw_w, dwb,
                                        clw, clb, lmw, lmb, state_hgrn, conf_tmajor, h_s, c_s)
        else:
            o = layer // 2
            xp, sb = _odd_prompt(xp, o, layer, sc_w_in, sc_w_out, sc_conv_w, lmw, lmb)
            s_p.append(sb)
            xs, sb = _odd_sample(xs, o, layer, sc_w_in, sc_w_out, sc_conv_w, lmw, lmb, sconv_flat)
            s_s.append(sb.reshape(n_seq, SC_KERNEL - 1, SC_WIDTH))
        xp, xs = _ffn(xp.reshape(bsz * t_len, d), xs, layer, ffn_w1, ffn_w3, ffn_w2, lfw, lfb)
        xp = xp.reshape(bsz, t_len, d)
    return (xp, xs.reshape(n_seq, 1, d), jnp.stack(h_p), jnp.stack(c_p), jnp.stack(s_p), h_s,
            jnp.transpose(c_s, (0, 2, 1, 3)), jnp.stack(s_s))
```

```python
import functools
import math

import numpy as np
import jax
import jax.numpy as jnp
from jax import lax
from jax.experimental import pallas as pl
from jax.experimental.pallas import tpu as pltpu

F32 = jnp.float32
BF16 = jnp.bfloat16

D_MODEL = 1024
DEPTH = 4
HGRN_WIDTH = D_MODEL // 2
HGRN_DK = 128
HGRN_DV = 128
HGRN_HEADS = HGRN_WIDTH // HGRN_DK
CONF_WIDTH = D_MODEL - HGRN_WIDTH
CONF_KERNEL = 31
SC_WIDTH = D_MODEL
SC_KERNEL = 3
D_FF = 256 * math.ceil(8 * D_MODEL / 3 / 256)
EVEN_IN = 4 * HGRN_WIDTH + 2 * CONF_WIDTH
ALPHA = (2 * DEPTH) ** 0.25
LOG2_E = math.log2(math.e)
LN_EPS = 1e-5
RMS_EPS = 1e-6

SUBLANES = 8
LANES = 128
CHUNK = 128
N_LEVELS = CHUNK.bit_length() - 1
N_MXU_LEVELS = SUBLANES.bit_length() - 1
TB_EVEN = 512
TB_ODD = 512
TM_FFN = 512
CONF_HIST = 32
SC_HIST = SUBLANES
ROW_PITCH = 3
SAMPLE_GROUP = 8
VMEM_LIMIT = 56 * 1024 * 1024


def _sigmoid(x):
    return 1.0 / (1.0 + jnp.exp(-x))


def _silu(x):
    return x * _sigmoid(x)


def _layernorm(x, w, b):
    mu = jnp.mean(x, axis=-1, keepdims=True)
    xc = x - mu
    var = jnp.mean(xc * xc, axis=-1, keepdims=True)
    return xc * lax.rsqrt(var + LN_EPS) * w + b


def _dot(a, b):
    return jnp.dot(a, b, preferred_element_type=F32)


def _dot_nt(a, b):
    return lax.dot_general(a, b, (((1,), (1,)), ((), ())), preferred_element_type=F32)


def _dot_tn(a, b):
    return lax.dot_general(a, b, (((0,), (0,)), ((), ())), preferred_element_type=F32)


def _split3(x):
    hi = x.astype(BF16)
    r1 = x - hi.astype(F32)
    mid = r1.astype(BF16)
    lo = (r1 - mid.astype(F32)).astype(BF16)
    return hi, mid, lo


def _lower_bound(logits, e):
    m = jnp.max(logits, axis=0, keepdims=True)
    ex = jnp.exp(logits - m)
    sm = ex / jnp.sum(ex, axis=0, keepdims=True)
    lb = jnp.zeros_like(m)
    for i in range(1, e + 1):
        lb = lb + sm[i:i + 1]
    return lb


def _hgrn_gates(zq, zf, lb):
    t = jnp.exp(-jnp.abs(zf))
    s = 1.0 + t
    r = 1.0 / s
    sig_neg = jnp.where(zf >= 0.0, t * r, r)
    log_sig = jnp.minimum(zf, 0.0) - jnp.log(s)
    a = jnp.log(lb)
    c = jnp.log1p(-lb) + log_sig
    logf = jnp.maximum(a, c) + jnp.log(1.0 + jnp.exp(-jnp.abs(a - c)))
    kk = (1.0 - lb) * sig_neg
    qq = _silu(zq) * (HGRN_DK ** -0.5)
    return qq, kk, logf


def _level_operand(q, k, e_j, j):
    L, width = q.shape
    half = 1 << (j - 1)
    if half >= SUBLANES:
        pieces = []
        for s in range(0, L, 2 * half):
            pieces.append(k[s:s + half])
            pieces.append(q[s + half:s + 2 * half])
        base = jnp.concatenate(pieces, axis=0)
    else:
        sub = lax.broadcasted_iota(jnp.int32, (1, SUBLANES, width), 1)
        upper = ((sub >> (j - 1)) & 1) == 1
        shape3 = (L // SUBLANES, SUBLANES, width)
        base = jnp.where(upper, q.reshape(shape3), k.reshape(shape3)).reshape(L, width)
    return (base * e_j).astype(BF16)


def _block_diag(a):
    z = jnp.zeros((a.shape[0], LANES), a.dtype)
    return jnp.concatenate([jnp.concatenate([a[:, :LANES], z], axis=1),
                            jnp.concatenate([z, a[:, LANES:]], axis=1)], axis=0)


def _chunk_constants():
    L = CHUNK
    t = np.arange(L)[:, None]
    r = np.arange(L)[None, :]
    mats = []
    for j in range(1, N_MXU_LEVELS + 1):
        half = 1 << (j - 1)
        start = (t >> j) << j
        mid = start + half - 1
        upper = t >= start + half
        d = np.where(upper, (r > mid) & (r <= t), (r > t) & (r <= mid))
        mats.append(d)
    mats.append(r <= t)
    d_all = np.concatenate(mats, axis=0).astype(np.float32)
    x = np.bitwise_xor(t, r)
    lvl = np.zeros((L, L), np.int32)
    nz = x > 0
    lvl[nz] = np.floor(np.log2(x[nz])).astype(np.int32) + 1
    lvl = np.where(r > t, -1, lvl).astype(np.int32)
    d_all = np.concatenate([d_all] * 3, axis=1)
    lvl = np.concatenate([lvl, lvl], axis=1)
    return jnp.asarray(d_all, dtype=BF16), jnp.asarray(lvl)


def _pitched(start, n):
    return pl.ds(ROW_PITCH * start, n, stride=ROW_PITCH)


def _zero_history(buf_ref, n_hist):
    for c in range(buf_ref.shape[0]):
        buf_ref[c, _pitched(0, n_hist), :] = jnp.zeros((n_hist, LANES), F32)


def _causal_dwconv(buf_ref, new_rows, w_ref, n_hist):
    tiles = [_causal_dwconv_tile(buf_ref, c, new_rows, w_ref, n_hist) for c in range(new_rows.shape[1] // LANES)]
    return jnp.concatenate([t[0] for t in tiles], axis=-1), jnp.concatenate([t[1] for t in tiles], axis=-1)


def _causal_dwconv_tile(buf_ref, c, new_rows, w_ref, n_hist):
    tb = new_rows.shape[0]
    taps = w_ref.shape[0]
    off = n_hist - (taps - 1)
    cols = slice(c * LANES, (c + 1) * LANES)
    buf_ref[c, _pitched(n_hist, tb), :] = new_rows[:, cols]
    acc = w_ref[taps - 1:taps, cols] * new_rows[:, cols]
    for j in range(taps - 1):
        acc = acc + w_ref[j:j + 1, cols] * buf_ref[c, _pitched(off + j, tb), :]
    hist = buf_ref[c, _pitched(tb, n_hist), :]
    buf_ref[c, _pitched(0, n_hist), :] = hist
    return acc, hist


def _even_prompt_kernel(e, n_t, x_ref, w_in_ref, w_out_ref, logit_ref, gn_ref, dww_ref, dwb_ref, clw_ref, clb_ref,
                        lnw_ref, lnb_ref, dall_ref, lvl_ref, y_ref, s_out_ref, conf_out_ref, st_ref, ubuf_ref):
    t_idx = pl.program_id(1)
    tb = x_ref.shape[0]
    hw = HGRN_WIDTH

    @pl.when(t_idx == 0)
    def _():
        st_ref[...] = jnp.zeros_like(st_ref)
        _zero_history(ubuf_ref, CONF_HIST)

    x = x_ref[...]
    xb = x.astype(BF16)
    L = CHUNK
    n_tiles = CONF_WIDTH // LANES

    pc = _dot(xb, w_in_ref[:, 4 * hw:])
    u = pc[:, :CONF_WIDTH] * _sigmoid(pc[:, CONF_WIDTH:])
    conv_tiles = [_causal_dwconv_tile(ubuf_ref, 0, u, dww_ref, CONF_HIST)]
    pq = _dot(xb, w_in_ref[:, 0:2 * hw])
    conv_tiles.append(_causal_dwconv_tile(ubuf_ref, 1, u, dww_ref, CONF_HIST))
    pv = _dot(xb, w_in_ref[:, 2 * hw:4 * hw])
    vi = pv[:, 0:hw].astype(BF16)
    zg = pv[:, hw:]

    def conv_finish():
        conv = jnp.concatenate([t[0] for t in conv_tiles], axis=-1)
        cc = _silu(_layernorm(conv + dwb_ref[...], clw_ref[...], clb_ref[...]))
        return _dot(cc.astype(BF16), w_out_ref[hw:, :])

    fill = [lambda c=c: conv_tiles.append(_causal_dwconv_tile(ubuf_ref, c, u, dww_ref, CONF_HIST))
            for c in range(2, n_tiles)] + [conv_finish]
    fill_out = []

    lb = _lower_bound(logit_ref[...], e)
    lvl = lvl_ref[...]
    dall = dall_ref[...]
    gn = gn_ref[...]
    n_pairs = HGRN_HEADS // 2

    def exponents(c):
        rows = slice(c * L, (c + 1) * L)
        qq, kk, logf = _hgrn_gates(pq[rows, 0:hw], pq[rows, hw:], lb)
        g3 = jnp.concatenate(_split3(logf * LOG2_E), axis=0)
        return qq, kk, _dot(dall, g3)

    def chunk(c, qq, kk, eb):
        rows = slice(c * L, (c + 1) * L)
        b = eb[N_MXU_LEVELS * L:, :]
        e_lvl = [jnp.exp2(eb[(j - 1) * L:j * L, :]) for j in range(1, N_MXU_LEVELS + 1)]
        for j in range(N_MXU_LEVELS + 1, N_LEVELS + 1):
            size, half = 1 << j, 1 << (j - 1)
            pieces = []
            for s in range(0, L, size):
                b_m = b[s + half - 1:s + half, :]
                pieces += [b_m - b[s:s + half, :], b[s + half:s + size, :] - b_m]
            e_lvl.append(jnp.exp2(jnp.concatenate(pieces, axis=0)))
        e_q_all = jnp.exp2(b)
        e_k_all = jnp.exp2(b[L - 1:L, :] - b)
        cols = [slice(2 * hp * HGRN_DK, 2 * (hp + 1) * HGRN_DK) for hp in range(n_pairs)]
        q2 = [qq[:, cs] for cs in cols]
        k2 = [kk[:, cs] for cs in cols]
        lvl_v = jnp.concatenate([lvl[:, :L], lvl[:, :L]], axis=0)
        scores = [jnp.where(lvl_v == 0, _dot_nt(_block_diag(q2[hp].astype(BF16)), k2[hp].astype(BF16)), 0.0)
                  for hp in range(n_pairs)]
        for j in range(1, N_LEVELS + 1):
            for hp in range(n_pairs):
                m_j = _level_operand(q2[hp], k2[hp], e_lvl[j - 1][:, cols[hp]], j)
                scores[hp] = jnp.where(lvl_v == j, _dot_nt(_block_diag(m_j), m_j), scores[hp])
        scores = [jnp.concatenate([sc[:L], sc[L:]], axis=1) for sc in scores]
        o_heads = []
        for hp in range(n_pairs):
            v2 = vi[rows, cols[hp]]
            e_q = e_q_all[:, cols[hp]]
            qd = (q2[hp] * e_q).astype(BF16)
            kd = (k2[hp] * e_k_all[:, cols[hp]]).astype(BF16)
            st = [st_ref[2 * hp], st_ref[2 * hp + 1]]
            st2 = jnp.concatenate([st[0].astype(BF16), st[1].astype(BF16)], axis=1)
            o_inter = _dot_nt(_block_diag(qd), st2)
            o2 = _dot(scores[hp].astype(BF16), _block_diag(v2)) + jnp.concatenate([o_inter[:L], o_inter[L:]], axis=1)
            for i in range(2):
                hc = slice(i * HGRN_DK, (i + 1) * HGRN_DK)
                st_ref[2 * hp + i] = e_q[L - 1:L, hc] * st[i] + _dot_tn(v2[:, hc], kd[:, hc])
                o_h = o2[:, hc]
                o_heads.append(o_h * lax.rsqrt(jnp.mean(o_h * o_h, axis=-1, keepdims=True) + RMS_EPS) * gn)
        return jnp.concatenate(o_heads, axis=-1) * _silu(zg[rows])

    n_chunks = tb // L
    o_chunks = []
    finished = 0

    def finish(upto, y_conv):
        rows = slice(finished * L, upto * L)
        o_rows = jnp.concatenate(o_chunks[finished:upto], axis=0)
        y = y_conv[rows] + _dot(o_rows.astype(BF16), w_out_ref[0:hw, :])
        y_ref[rows, :] = _layernorm(ALPHA * x[rows] + y, lnw_ref[...], lnb_ref[...])

    ahead = exponents(0)
    for c in range(n_chunks):
        cur = ahead
        if c + 1 < n_chunks:
            ahead = exponents(c + 1)
        o_chunks.append(chunk(c, *cur))
        if fill:
            fill_out.append(fill.pop(0)())
        if len(fill) == 1 and c + 1 < n_chunks:
            fill_out.append(fill.pop(0)())
        if not fill and c + 1 == n_chunks // 2 + 1 and n_chunks >= 4:
            finish(n_chunks // 2, fill_out[-1])
            finished = n_chunks // 2
    while fill:
        fill_out.append(fill.pop(0)())
    finish(n_chunks, fill_out[-1])

    @pl.when(t_idx == n_t - 1)
    def _():
        hist = jnp.concatenate([t[1] for t in conv_tiles], axis=-1)
        conf_out_ref[...] = hist[CONF_HIST - (CONF_KERNEL - 1):, :]
        for h in range(HGRN_HEADS):
            s_out_ref[h] = st_ref[h].T


def _even_prompt(x, e, layer, w_in, w_out, logits, gn, dww, dwb, clw, clb, lnw, lnb, dall, lvl):
    bsz, t_len, d = x.shape
    tb = min(TB_EVEN, t_len)
    assert t_len % tb == 0 and tb % CHUNK == 0 and t_len >= CONF_KERNEL - 1
    n_t = t_len // tb
    const2 = lambda b, t: (0, 0)
    return pl.pallas_call(
        functools.partial(_even_prompt_kernel, e, n_t),
        grid=(bsz, n_t),
        in_specs=[
            pl.BlockSpec((None, tb, d), lambda b, t: (b, t, 0)),
            pl.BlockSpec((None, d, EVEN_IN), lambda b, t: (e, 0, 0)),
            pl.BlockSpec((None, d, d), lambda b, t: (e, 0, 0)),
            pl.BlockSpec(logits.shape, const2),
            pl.BlockSpec((None, 1, HGRN_DV), lambda b, t: (e, 0, 0)),
            pl.BlockSpec((None, CONF_KERNEL, CONF_WIDTH), lambda b, t: (e, 0, 0)),
            pl.BlockSpec((None, 1, CONF_WIDTH), lambda b, t: (e, 0, 0)),
            pl.BlockSpec((None, 1, CONF_WIDTH), lambda b, t: (e, 0, 0)),
            pl.BlockSpec((None, 1, CONF_WIDTH), lambda b, t: (e, 0, 0)),
            pl.BlockSpec((None, 1, d), lambda b, t: (layer, 0, 0)),
            pl.BlockSpec((None, 1, d), lambda b, t: (layer, 0, 0)),
            pl.BlockSpec(dall.shape, const2),
            pl.BlockSpec(lvl.shape, const2),
        ],
        out_specs=[
            pl.BlockSpec((None, tb, d), lambda b, t: (b, t, 0)),
            pl.BlockSpec((None, HGRN_HEADS, HGRN_DK, HGRN_DV), lambda b, t: (b, 0, 0, 0)),
            pl.BlockSpec((None, CONF_KERNEL - 1, CONF_WIDTH), lambda b, t: (b, 0, 0)),
        ],
        out_shape=[
            jax.ShapeDtypeStruct((bsz, t_len, d), F32),
            jax.ShapeDtypeStruct((bsz, HGRN_HEADS, HGRN_DK, HGRN_DV), F32),
            jax.ShapeDtypeStruct((bsz, CONF_KERNEL - 1, CONF_WIDTH), F32),
        ],
        scratch_shapes=[
            pltpu.VMEM((HGRN_HEADS, HGRN_DV, HGRN_DK), F32),
            pltpu.VMEM((CONF_WIDTH // LANES, ROW_PITCH * (CONF_HIST + tb), LANES), F32),
        ],
        compiler_params=pltpu.CompilerParams(dimension_semantics=("arbitrary", "arbitrary"),
                                             vmem_limit_bytes=VMEM_LIMIT),
        name=f"even_prompt_{e}",
    )(x, w_in, w_out, logits, gn, dww, dwb, clw, clb, lnw, lnb, dall, lvl)


def _even_sample_kernel(e, n_g, aliased, x_ref, w_in_ref, w_out_ref, logit_ref, gn_ref, dww_ref, dwb_ref, clw_ref,
                        clb_ref, lnw_ref, lnb_ref, s_ref, c_ref, *rest):
    if aliased:
        rest = rest[2:]
    y_ref, s_out_ref, c_out_ref, qt_ref, at_ref, kt_ref, v_ref, gate_ref, u_ref, o_ref, cacc_ref = rest
    i = pl.program_id(0)
    hw = HGRN_WIDTH
    n_seq = x_ref.shape[0]

    @pl.when(i == 0)
    def _():
        p = _dot(x_ref[...].astype(BF16), w_in_ref[...])
        lb = _lower_bound(logit_ref[...], e)
        qq, kk, logf = _hgrn_gates(p[:, 0:hw], p[:, hw:2 * hw], lb)
        qt_ref[...] = qq.T
        kt_ref[...] = kk.T
        at_ref[...] = jnp.exp(logf).T
        v_ref[...] = p[:, 2 * hw:3 * hw]
        gate_ref[...] = _silu(p[:, 3 * hw:4 * hw])
        u_ref[...] = p[:, 4 * hw:4 * hw + CONF_WIDTH] * _sigmoid(p[:, 4 * hw + CONF_WIDTH:])

    lane = lax.broadcasted_iota(jnp.int32, (HGRN_DK, n_seq), 1)
    n_hist = CONF_KERNEL - 1
    grp = pl.ds(pl.multiple_of(i * SAMPLE_GROUP, SAMPLE_GROUP), SAMPLE_GROUP)
    v_grp = v_ref[grp, :]
    u_grp = u_ref[grp, :]

    conv = dww_ref[n_hist:n_hist + 1, :] * u_grp + dwb_ref[...]
    for j in range(n_hist):
        conv = conv + dww_ref[j:j + 1, :] * c_ref[j]
    cacc_ref[grp, :] = conv
    c_out_ref[0:n_hist - 1] = c_ref[1:n_hist]
    c_out_ref[n_hist - 1] = u_grp

    o_rows = []
    for j in range(SAMPLE_GROUP):
        pick = lane == i * SAMPLE_GROUP + j
        o_heads = []
        for h in range(HGRN_HEADS):
            cols = slice(h * HGRN_DK, (h + 1) * HGRN_DK)
            a_col = jnp.sum(jnp.where(pick, at_ref[cols, :], 0.0), axis=1, keepdims=True)
            k_col = jnp.sum(jnp.where(pick, kt_ref[cols, :], 0.0), axis=1, keepdims=True)
            q_col = jnp.sum(jnp.where(pick, qt_ref[cols, :], 0.0), axis=1, keepdims=True)
            s_new = a_col * s_ref[j, h] + k_col * v_grp[j:j + 1, cols]
            s_out_ref[j, h] = s_new
            o_heads.append(jnp.sum(q_col * s_new, axis=0, keepdims=True))
        o_rows.append(jnp.concatenate(o_heads, axis=-1))
    o_ref[grp, :] = jnp.concatenate(o_rows, axis=0)

    @pl.when(i == n_g - 1)
    def _():
        gn = gn_ref[...]
        o_heads = []
        for h in range(HGRN_HEADS):
            o_h = o_ref[:, h * HGRN_DK:(h + 1) * HGRN_DK]
            o_heads.append(o_h * lax.rsqrt(jnp.mean(o_h * o_h, axis=-1, keepdims=True) + RMS_EPS) * gn)
        o = jnp.concatenate(o_heads, axis=-1) * gate_ref[...]
        cc = _silu(_layernorm(cacc_ref[...], clw_ref[...], clb_ref[...]))
        y = _dot(jnp.concatenate([o, cc], axis=-1).astype(BF16), w_out_ref[...])
        y_ref[...] = _layernorm(ALPHA * x_ref[...] + y, lnw_ref[...], lnb_ref[...])


def _even_sample(x, e, layer, w_in, w_out, logits, gn, dww, dwb, clw, clb, lnw, lnb, state_hgrn, state_conf,
                 prev_h=None, prev_c=None):
    n_seq, d = x.shape
    g = SAMPLE_GROUP
    assert n_seq % g == 0 and n_seq % 128 == 0
    n_g = n_seq // g
    aliased = prev_h is not None
    const2 = lambda i: (0, 0)
    in_specs = [
        pl.BlockSpec((n_seq, d), const2),
        pl.BlockSpec((None, d, EVEN_IN), lambda i: (e, 0, 0)),
        pl.BlockSpec((None, d, d), lambda i: (e, 0, 0)),
        pl.BlockSpec(logits.shape, const2),
        pl.BlockSpec((None, 1, HGRN_DV), lambda i: (e, 0, 0)),
        pl.BlockSpec((None, CONF_KERNEL, CONF_WIDTH), lambda i: (e, 0, 0)),
        pl.BlockSpec((None, 1, CONF_WIDTH), lambda i: (e, 0, 0)),
        pl.BlockSpec((None, 1, CONF_WIDTH), lambda i: (e, 0, 0)),
        pl.BlockSpec((None, 1, CONF_WIDTH), lambda i: (e, 0, 0)),
        pl.BlockSpec((None, 1, d), lambda i: (layer, 0, 0)),
        pl.BlockSpec((None, 1, d), lambda i: (layer, 0, 0)),
        pl.BlockSpec((None, g, HGRN_HEADS, HGRN_DK, HGRN_DV), lambda i: (e, i, 0, 0, 0)),
        pl.BlockSpec((None, CONF_KERNEL - 1, g, CONF_WIDTH), lambda i: (e, 0, i, 0)),
    ]
    args =[x, w_in, w_out, logits, gn, dww, dwb, clw, clb, lnw, lnb, state_hgrn, state_conf]
    aliases = {}
    if aliased:
        in_specs += [pl.BlockSpec(memory_space=pl.ANY), pl.BlockSpec(memory_space=pl.ANY)]
        aliases = {len(args): 1, len(args) + 1: 2}
        args += [prev_h, prev_c]
    return pl.pallas_call(
        functools.partial(_even_sample_kernel, e, n_g, aliased),
        grid=(n_g,),
        in_specs=in_specs,
        out_specs=[
            pl.BlockSpec((n_seq, d), const2),
            pl.BlockSpec((None, g, HGRN_HEADS, HGRN_DK, HGRN_DV), lambda i: (e, i, 0, 0, 0)),
            pl.BlockSpec((None, CONF_KERNEL - 1, g, CONF_WIDTH), lambda i: (e, 0, i, 0)),
        ],
        out_shape=[
            jax.ShapeDtypeStruct((n_seq, d), F32),
            jax.ShapeDtypeStruct(state_hgrn.shape, F32),
            jax.ShapeDtypeStruct(state_conf.shape, F32),
        ],
        scratch_shapes=[
            pltpu.VMEM((HGRN_WIDTH, n_seq), F32),
            pltpu.VMEM((HGRN_WIDTH, n_seq), F32),
            pltpu.VMEM((HGRN_WIDTH, n_seq), F32),
            pltpu.VMEM((n_seq, HGRN_WIDTH), F32),
            pltpu.VMEM((n_seq, HGRN_WIDTH), F32),
            pltpu.VMEM((n_seq, CONF_WIDTH), F32),
            pltpu.VMEM((n_seq, HGRN_WIDTH), F32),
            pltpu.VMEM((n_seq, CONF_WIDTH), F32),
        ],
        input_output_aliases=aliases,
        compiler_params=pltpu.CompilerParams(dimension_semantics=("arbitrary",), vmem_limit_bytes=VMEM_LIMIT),
        name=f"even_sample_{e}",
    )(*args)


def _odd_prompt_kernel(n_t, x_ref, w_in_ref, w_out_ref, cw_ref, lnw_ref, lnb_ref, y_ref, z_out_ref, zbuf_ref):
    t_idx = pl.program_id(1)
    tb = x_ref.shape[0]
    w = SC_WIDTH

    @pl.when(t_idx == 0)
    def _():
        _zero_history(zbuf_ref, SC_HIST)

    x = x_ref[...]
    xb = x.astype(BF16)
    z = _dot(xb, w_in_ref[:, w:2 * w]) * _dot(xb, w_in_ref[:, 2 * w:3 * w])
    conv, hist = _causal_dwconv(zbuf_ref, z, cw_ref, SC_HIST)
    gated = (_dot(xb, w_in_ref[:, 0:w]) * conv).astype(BF16)
    n_groups = 2 if tb % (2 * SUBLANES) == 0 else 1
    m = tb // n_groups
    for r in range(n_groups):
        rows = slice(r * m, (r + 1) * m)
        y_ref[rows, :] = _layernorm(ALPHA * x[rows] + _dot(gated[rows], w_out_ref[...]), lnw_ref[...], lnb_ref[...])

    @pl.when(t_idx == n_t - 1)
    def _():
        z_out_ref[...] = hist[SC_HIST - (SC_KERNEL - 1):, :]


def _odd_prompt(x, o, layer, w_in, w_out, cw, lnw, lnb):
    bsz, t_len, d = x.shape
    tb = min(TB_ODD, t_len)
    assert t_len % tb == 0 and t_len >= SC_KERNEL - 1
    n_t = t_len // tb
    return pl.pallas_call(
        functools.partial(_odd_prompt_kernel, n_t),
        grid=(bsz, n_t),
        in_specs=[
            pl.BlockSpec((None, tb, d), lambda b, t: (b, t, 0)),
            pl.BlockSpec((None, d, 3 * SC_WIDTH), lambda b, t: (o, 0, 0)),
            pl.BlockSpec((None, SC_WIDTH, d), lambda b, t: (o, 0, 0)),
            pl.BlockSpec((None, SC_KERNEL, SC_WIDTH), lambda b, t: (o, 0, 0)),
            pl.BlockSpec((None, 1, d), lambda b, t: (layer, 0, 0)),
            pl.BlockSpec((None, 1, d), lambda b, t: (layer, 0, 0)),
        ],
        out_specs=[
            pl.BlockSpec((None, tb, d), lambda b, t: (b, t, 0)),
            pl.BlockSpec((None, SC_KERNEL - 1, SC_WIDTH), lambda b, t: (b, 0, 0)),
        ],
        out_shape=[
            jax.ShapeDtypeStruct((bsz, t_len, d), F32),
            jax.ShapeDtypeStruct((bsz, SC_KERNEL - 1, SC_WIDTH), F32),
        ],
        scratch_shapes=[pltpu.VMEM((SC_WIDTH // LANES, ROW_PITCH * (SC_HIST + tb), LANES), F32)],
        compiler_params=pltpu.CompilerParams(dimension_semantics=("arbitrary", "arbitrary"),
                                             vmem_limit_bytes=VMEM_LIMIT),
        name=f"odd_prompt_{o}",
    )(x, w_in, w_out, cw, lnw, lnb)


def _odd_sample_kernel(x_ref, w_in_ref, w_out_ref, cw_ref, lnw_ref, lnb_ref, s_ref, y_ref, s_out_ref):
    w = SC_WIDTH
    x = x_ref[...]
    p = _dot(x.astype(BF16), w_in_ref[...])
    z = p[:, w:2 * w] * p[:, 2 * w:3 * w]
    s0 = s_ref[:, 0:w]
    s1 = s_ref[:, w:2 * w]
    conv = cw_ref[0:1, :] * s0 + cw_ref[1:2, :] * s1 + cw_ref[2:3, :] * z
    s_out_ref[:, 0:w] = s1
    s_out_ref[:, w:2 * w] = z
    y = _dot((p[:, 0:w] * conv).astype(BF16), w_out_ref[...])
    y_ref[...] = _layernorm(ALPHA * x + y, lnw_ref[...], lnb_ref[...])


def _odd_sample(x, o, layer, w_in, w_out, cw, lnw, lnb, state):
    n_seq, d = x.shape
    assert SC_KERNEL == 3
    hist = (SC_KERNEL - 1) * SC_WIDTH
    return pl.pallas_call(
        _odd_sample_kernel,
        grid=(1,),
        in_specs=[
            pl.BlockSpec((n_seq, d), lambda i: (0, 0)),
            pl.BlockSpec((None, d, 3 * SC_WIDTH), lambda i: (o, 0, 0)),
            pl.BlockSpec((None, SC_WIDTH, d), lambda i: (o, 0, 0)),
            pl.BlockSpec((None, SC_KERNEL, SC_WIDTH), lambda i: (o, 0, 0)),
            pl.BlockSpec((None, 1, d), lambda i: (layer, 0, 0)),
            pl.BlockSpec((None, 1, d), lambda i: (layer, 0, 0)),
            pl.BlockSpec((None, n_seq, hist), lambda i: (o, 0, 0)),
        ],
        out_specs=[
            pl.BlockSpec((n_seq, d), lambda i: (0, 0)),
            pl.BlockSpec((n_seq, hist), lambda i: (0, 0)),
        ],
        out_shape=[
            jax.ShapeDtypeStruct((n_seq, d), F32),
            jax.ShapeDtypeStruct((n_seq, hist), F32),
        ],
        compiler_params=pltpu.CompilerParams(dimension_semantics=("arbitrary",), vmem_limit_bytes=VMEM_LIMIT),
        name=f"odd_sample_{o}",
    )(x, w_in, w_out, cw, lnw, lnb, state)


def _ffn_kernel(n_p, xp_ref, xs_ref, w1_ref, w3_ref, w2_ref, lnw_ref, lnb_ref, yp_ref, ys_ref, pre_ref):
    i = pl.program_id(0)

    def residual_swiglu(x):
        xb = x.astype(BF16)
        h1 = _dot(xb, w1_ref[...])
        h3 = _dot(xb, w3_ref[...])
        return ALPHA * x + _dot((_silu(h1) * h3).astype(BF16), w2_ref[...])

    @pl.when(i == 0)
    def _():
        pre_ref[...] = jnp.zeros_like(pre_ref)

    @pl.when(i < n_p)
    def _():
        yp_ref[...] = _layernorm(pre_ref[...], lnw_ref[...], lnb_ref[...])
        pre_ref[...] = residual_swiglu(xp_ref[...])

    @pl.when(i == n_p)
    def _():
        yp_ref[...] = _layernorm(pre_ref[...], lnw_ref[...], lnb_ref[...])
        ys_ref[...] = _layernorm(residual_swiglu(xs_ref[...]), lnw_ref[...], lnb_ref[...])


def _ffn(xp, xs, layer, w1, w3, w2, lnw, lnb):
    m, d = xp.shape
    tm = min(TM_FFN, m)
    assert m % tm == 0
    n_p = m // tm
    prompt_block = pl.BlockSpec((tm, d), lambda i: (jnp.minimum(i, n_p - 1), 0))
    sample_block = pl.BlockSpec(xs.shape, lambda i: (0, 0))
    return pl.pallas_call(
        functools.partial(_ffn_kernel, n_p),
        grid=(n_p + 1,),
        in_specs=[
            prompt_block,
            sample_block,
            pl.BlockSpec((None, d, D_FF), lambda i: (layer, 0, 0)),
            pl.BlockSpec((None, d, D_FF), lambda i: (layer, 0, 0)),
            pl.BlockSpec((None, D_FF, d), lambda i: (layer, 0, 0)),
            pl.BlockSpec((None, 1, d), lambda i: (layer, 0, 0)),
            pl.BlockSpec((None, 1, d), lambda i: (layer, 0, 0)),
        ],
        out_specs=[pl.BlockSpec((tm, d), lambda i: (jnp.maximum(i - 1, 0), 0)), sample_block],
        out_shape=[jax.ShapeDtypeStruct((m, d), F32), jax.ShapeDtypeStruct(xs.shape, F32)],
        scratch_shapes=[pltpu.VMEM((tm, d), F32)],
        compiler_params=pltpu.CompilerParams(dimension_semantics=("arbitrary",), vmem_limit_bytes=VMEM_LIMIT),
        name=f"ffn_{layer}",
    )(xp, xs, w1, w3, w2, lnw, lnb)


def kernel(x_prompt, x_sample, state_hgrn, state_conf, state_sconv, w_in_even, w_out_even, hgrn_lb_logits,
           hgrn_gnorm_w, conf_dw_w, conf_dw_b, conf_ln_w, conf_ln_b, sc_w_in, sc_conv_w, sc_w_out,
           ffn_w1, ffn_w3, ffn_w2, ln_mix_w, ln_mix_b, ln_ffn_w, ln_ffn_b):
    bsz, t_len, d = x_prompt.shape
    n_seq = x_sample.shape[0]
    n_even, n_odd = state_hgrn.shape[0], state_sconv.shape[0]
    assert d == D_MODEL and x_sample.shape[1] == 1

    w_in_even = w_in_even.astype(BF16)
    w_out_even = w_out_even.astype(BF16)
    sc_w_in = sc_w_in.astype(BF16)
    sc_w_out = sc_w_out.astype(BF16)
    ffn_w1 = ffn_w1.astype(BF16)
    ffn_w3 = ffn_w3.astype(BF16)
    ffn_w2 = ffn_w2.astype(BF16)
    row = lambda a: a.reshape(a.shape[0], 1, a.shape[1])
    gn, dwb, clw, clb = row(hgrn_gnorm_w), row(conf_dw_b), row(conf_ln_w), row(conf_ln_b)
    lmw, lmb, lfw, lfb = row(ln_mix_w), row(ln_mix_b), row(ln_ffn_w), row(ln_ffn_b)
    sconv_flat = state_sconv.reshape(n_odd, n_seq, (SC_KERNEL - 1) * SC_WIDTH)
    conf_tmajor = jnp.transpose(state_conf, (0, 2, 1, 3))
    dall, lvl = _chunk_constants()

    xp = x_prompt
    xs = x_sample.reshape(n_seq, d)
    h_p, c_p, s_p, s_s = [], [], [], []
    h_s = c_s = None
    for layer in range(DEPTH):
        if layer % 2 == 0:
            e = layer // 2
            xp, sh, cb = _even_prompt(xp, e, layer, w_in_even, w_out_even, hgrn_lb_logits, gn, conf_dw_w, dwb,
                                      clw, clb, lmw, lmb, dall, lvl)
            h_p.append(sh)
            c_p.append(cb)
            xs, h_s, c_s = _even_sample(xs, e, layer, w_in_even, w_out_even, hgrn_lb_logits, gn, conf_dw_w, dwb,
                                        clw, clb, lmw, lmb, state_hgrn, conf_tmajor, h_s, c_s)
        else:
            o = layer // 2
            xp, sb = _odd_prompt(xp, o, layer, sc_w_in, sc_w_out, sc_conv_w, lmw, lmb)
            s_p.append(sb)
            xs, sb = _odd_sample(xs, o, layer, sc_w_in, sc_w_out, sc_conv_w, lmw, lmb, sconv_flat)
            s_s.append(sb.reshape(n_seq, SC_KERNEL - 1, SC_WIDTH))
        xp, xs = _ffn(xp.reshape(bsz * t_len, d), xs, layer, ffn_w1, ffn_w3, ffn_w2, lfw, lfb)
        xp = xp.reshape(bsz, t_len, d)
    return (xp, xs.reshape(n_seq, 1, d), jnp.stack(h_p), jnp.stack(c_p), jnp.stack(s_p), h_s,
            jnp.transpose(c_s, (0, 2, 1, 3)), jnp.stack(s_s))
```
